```python
import math
import jax, jax.numpy as jnp
from jax import lax
import numpy as np

D_MODEL = 1024
BATCH = 4
SEQ = 4096
DEPTH = 2
DEC_BATCH = 16
DEC_SEQ = 16
PAST_LEN = 4096

CHUNK = 64
Q_BLOCK = 128
D_MIX = D_MODEL
HEAD_DIM = 64
SB_HEADS = D_MIX // (4 * HEAD_DIM)
SB_WIDTH = SB_HEADS * HEAD_DIM
FOX_HEADS = D_MIX // (4 * HEAD_DIM)
FOX_WIDTH = FOX_HEADS * HEAD_DIM
SSD_INNER = D_MIX - SB_WIDTH - FOX_WIDTH
SSD_HEAD_DIM = 64
SSD_HEADS = SSD_INNER // SSD_HEAD_DIM
SSD_GROUPS = 2
SSD_HEADS_PER_GROUP = SSD_HEADS // SSD_GROUPS
SSD_STATE = 128
SSD_CONV = 4
SSD_CHUNK = CHUNK
CONV_DIM = SSD_INNER + 2 * SSD_GROUPS * SSD_STATE
IN_SIZES = (SB_WIDTH, SB_WIDTH, SB_WIDTH, FOX_WIDTH, FOX_WIDTH, FOX_WIDTH, FOX_HEADS, SSD_INNER, CONV_DIM, SSD_HEADS)
N_IN = 3 * SB_WIDTH + 3 * FOX_WIDTH + FOX_HEADS + SSD_INNER + CONV_DIM + SSD_HEADS
MEM_LEN = 256
X_HEADS = 4
X_HEAD_DIM = D_MODEL // X_HEADS
D_FF = ((8 * D_MODEL + 3 * 256 - 1) // (3 * 256)) * 256
EPS = 1e-6

kernel_name = 'hybrid_sb_ssd_fox_stream_step'


def rmsnorm(x, g):
    xf = x.astype(jnp.float32)
    y = xf * lax.rsqrt(jnp.mean(xf * xf, axis=-1, keepdims=True) + EPS)
    return (y * g.astype(jnp.float32)).astype(x.dtype)


def split_last(a, sizes):
    idx = np.cumsum(sizes)[:-1].tolist()
    return jnp.split(a, idx, axis=-1)


def query_sweep(fn, q_args, n):
    nb = n // Q_BLOCK
    blocks = tuple(jnp.moveaxis(a.reshape(a.shape[0], nb, Q_BLOCK, *a.shape[2:]), 1, 0) for a in q_args)
    pos = jnp.arange(n, dtype=jnp.int32).reshape(nb, Q_BLOCK)
    out = lax.map(lambda xs: fn(*xs), blocks + (pos,))
    out = jnp.moveaxis(out, 0, 1)
    return out.reshape(out.shape[0], n, *out.shape[3:])


def sb_attend(q, k, v, q_pos, k_pos):
    z = jnp.einsum('bqhd,bkhd->bhqk', q, k).astype(jnp.float32) * (HEAD_DIM ** -0.5)
    valid = k_pos[None, :] < q_pos[:, None]
    log_keep = jnp.where(valid, jax.nn.log_sigmoid(-z), 0.0)
    after = lax.cumsum(log_keep, axis=3, reverse=True) - log_keep
    w = jnp.where(valid, jnp.exp(jax.nn.log_sigmoid(z) + after), 0.0)
    return jnp.einsum('bhqk,bkhd->bqhd', w, v.astype(jnp.float32)).astype(q.dtype)


def fox_attend(q, k, v, fq, fk, q_pos, k_pos):
    s = jnp.einsum('bqhd,bkhd->bhqk', q, k).astype(jnp.float32) * (HEAD_DIM ** -0.5)
    s = s + (jnp.transpose(fq, (0, 2, 1))[..., :, None] - jnp.transpose(fk, (0, 2, 1))[..., None, :])
    valid = k_pos[None, :] <= q_pos[:, None]
    p = jax.nn.softmax(jnp.where(valid, s, -jnp.inf), axis=-1)
    return jnp.einsum('bhqk,bkhd->bqhd', p, v.astype(jnp.float32)).astype(q.dtype)


def causal_dwconv(xpad, w, b):
    y = lax.conv_general_dilated(xpad, w[:, None, :].astype(xpad.dtype), (1,), 'VALID',
                                 dimension_numbers=('NWC', 'WIO', 'NWC'),
                                 feature_group_count=xpad.shape[-1])
    return y + b.astype(xpad.dtype)


def ssd_scan(x, a, B, C, h0, chunk):
    b, n, nh, hp = x.shape
    ns = B.shape[-1]
    nc = n // chunk
    x = x.reshape(b, nc, chunk, nh, hp)
    a = a.reshape(b, nc, chunk, nh)
    B = B.reshape(b, nc, chunk, nh, ns)
    C = C.reshape(b, nc, chunk, nh, ns)
    a_cum = jnp.cumsum(a, axis=2)
    causal = jnp.tril(jnp.ones((chunk, chunk), dtype=bool))
    seg = a_cum[:, :, :, None, :] - a_cum[:, :, None, :, :]
    decay = jnp.exp(jnp.where(causal[None, None, :, :, None], seg, -jnp.inf))
    scores = jnp.einsum('bcthn,bcshn->bctsh', C, B) * decay
    y_diag = jnp.einsum('bctsh,bcshp->bcthp', scores, x)
    to_end = jnp.exp(a_cum[:, :, -1:, :] - a_cum)
    chunk_states = jnp.einsum('bcshn,bcsh,bcshp->bchpn', B, to_end, x)
    chunk_decay = jnp.exp(a_cum[:, :, -1, :])

    def step(hc, inp):
        dec, st = inp
        return dec[:, :, None, None] * hc + st, hc

    h_final, h_in = lax.scan(step, h0, (jnp.moveaxis(chunk_decay, 1, 0), jnp.moveaxis(chunk_states, 1, 0)))
    h_in = jnp.moveaxis(h_in, 0, 1)
    y_off = jnp.einsum('bcthn,bchpn->bcthp', C, h_in) * jnp.exp(a_cum)[..., None]
    return (y_diag + y_off).reshape(b, n, nh, hp), h_final


def mixers(u, p, past):
    b, n, _ = u.shape
    proj = jnp.einsum('bnd,de->bne', u, p['w_in'])
    sq, sk, sv, fq, fk, fv, fg, z, xbc, dt = split_last(proj, IN_SIZES)
    heads = lambda t: t.reshape(b, n, -1, HEAD_DIM)
    sq, sk, sv, fq, fk, fv = map(heads, (sq, sk, sv, fq, fk, fv))
    logf = jax.nn.log_sigmoid((fg + p['fox_b_f']).astype(jnp.float32))
    if past is None:
        kpos = jnp.arange(n, dtype=jnp.int32)
        sb_o = query_sweep(lambda qb, qp: sb_attend(qb, sk, sv, qp, kpos), (sq,), n)
        fcum = jnp.cumsum(logf, axis=1)
        fox_o = query_sweep(lambda qb, fb, qp: fox_attend(qb, fk, fv, fb, fcum, qp, kpos), (fq, fcum), n)
        xbc_pad = jnp.concatenate([jnp.zeros((b, SSD_CONV - 1, CONV_DIM), xbc.dtype), xbc], axis=1)
        h0 = jnp.zeros((b, SSD_HEADS, SSD_HEAD_DIM, SSD_STATE), jnp.float32)
        ssd_chunk = SSD_CHUNK
    else:
        past_len = past['sb_k'].shape[1]
        kpos = jnp.arange(past_len + n, dtype=jnp.int32)
        qpos = past_len + jnp.arange(n, dtype=jnp.int32)
        sb_k = jnp.concatenate([past['sb_k'].astype(sk.dtype), sk], axis=1)
        sb_v = jnp.concatenate([past['sb_v'].astype(sv.dtype), sv], axis=1)
        sb_o = sb_attend(sq, sb_k, sb_v, qpos, kpos)
        fcum = jnp.cumsum(jnp.concatenate([past['fox_logf'].astype(jnp.float32), logf], axis=1), axis=1)
        fox_k = jnp.concatenate([past['fox_k'].astype(fk.dtype), fk], axis=1)
        fox_v = jnp.concatenate([past['fox_v'].astype(fv.dtype), fv], axis=1)
        fox_o = fox_attend(fq, fox_k, fox_v, fcum[:, past_len:], fcum, qpos, kpos)
        xbc_pad = jnp.concatenate([past['conv'].astype(xbc.dtype), xbc], axis=1)
        h0 = past['ssm'].astype(jnp.float32)
        ssd_chunk = n
    conv_state = xbc_pad[:, -(SSD_CONV - 1):]
    xbc_c = jax.nn.silu(causal_dwconv(xbc_pad, p['conv_w'], p['conv_b']))
    xs, Bm, Cm = split_last(xbc_c, (SSD_INNER, SSD_GROUPS * SSD_STATE, SSD_GROUPS * SSD_STATE))
    xs = xs.reshape(b, n, SSD_HEADS, SSD_HEAD_DIM).astype(jnp.float32)
    Bm = jnp.repeat(Bm.reshape(b, n, SSD_GROUPS, SSD_STATE), SSD_HEADS_PER_GROUP, axis=2).astype(jnp.float32)
    Cm = jnp.repeat(Cm.reshape(b, n, SSD_GROUPS, SSD_STATE), SSD_HEADS_PER_GROUP, axis=2).astype(jnp.float32)
    dt = jax.nn.softplus((dt + p['dt_bias']).astype(jnp.float32))
    A = -jnp.exp(p['a_log'].astype(jnp.float32))
    y, h_final = ssd_scan(xs * dt[..., None], dt * A, Bm, Cm, h0, ssd_chunk)
    y = (y + p['d_skip'].astype(jnp.float32)[:, None] * xs).reshape(b, n, SSD_INNER).astype(u.dtype)
    ssd_o = rmsnorm(y * jax.nn.silu(z), p['ssd_norm_g'])
    mix = jnp.concatenate([sb_o.reshape(b, n, SB_WIDTH), fox_o.reshape(b, n, FOX_WIDTH), ssd_o], axis=-1)
    out = jnp.einsum('bne,ed->bnd', mix, p['w_out'])
    return out, (sk, sv, fk, fv, logf, h_final, conv_state)


def mem_kv(mem, g, wk, wv):
    b, m, _ = mem.shape
    mn = rmsnorm(mem, g)
    k = jnp.einsum('bmd,de->bme', mn, wk).reshape(b, m, X_HEADS, X_HEAD_DIM)
    v = jnp.einsum('bmd,de->bme', mn, wv).reshape(b, m, X_HEADS, X_HEAD_DIM)
    return k, v


def cross_attend(u, mk, mv, wq, wo):
    b, n, _ = u.shape
    q = jnp.einsum('bnd,de->bne', u, wq).reshape(b, n, X_HEADS, X_HEAD_DIM)
    s = jnp.einsum('bnhd,bmhd->bhnm', q, mk.astype(q.dtype)).astype(jnp.float32) * (X_HEAD_DIM ** -0.5)
    pr = jax.nn.softmax(s, axis=-1)
    o = jnp.einsum('bhnm,bmhd->bnhd', pr, mv.astype(jnp.float32)).astype(u.dtype)
    return jnp.einsum('bne,ed->bnd', o.reshape(b, n, D_MODEL), wo)


def swiglu(u, wg, wu, wd):
    hdn = jax.nn.silu(jnp.einsum('bnd,df->bnf', u, wg)) * jnp.einsum('bnd,df->bnf', u, wu)
    return jnp.einsum('bnf,fd->bnd', hdn, wd)


def layer(h, p, past, mk, mv):
    mix, st = mixers(rmsnorm(h, p['norm_mix_g']), p, past)
    h = h + mix.astype(h.dtype)
    h = h + cross_attend(rmsnorm(h, p['norm_x_g']), mk, mv, p['wq_x'], p['wo_x']).astype(h.dtype)
    h = h + swiglu(rmsnorm(h, p['norm_ffn_g']), p['w_gate'], p['w_up'], p['w_down']).astype(h.dtype)
    return h, st


def setup_inputs(seed: int = 0) -> dict:
    key = jax.random.key(seed)
    counter = [0]

    def nk():
        counter[0] += 1
        return jax.random.fold_in(key, counter[0])

    def nrm(shape, scale):
        return jax.random.normal(nk(), shape, jnp.float32) * scale

    def gain(shape):
        return 1.0 + nrm(shape, 0.02)

    x_prompt = nrm((BATCH, SEQ, D_MODEL), 1.0)
    x_sample = nrm((DEC_BATCH, DEC_SEQ, D_MODEL), 1.0)
    cache_sb_k = nrm((DEPTH, DEC_BATCH, PAST_LEN, SB_HEADS, HEAD_DIM), 1.0)
    cache_sb_v = nrm((DEPTH, DEC_BATCH, PAST_LEN, SB_HEADS, HEAD_DIM), 1.0)
    cache_fox_k = nrm((DEPTH, DEC_BATCH, PAST_LEN, FOX_HEADS, HEAD_DIM), 1.0)
    cache_fox_v = nrm((DEPTH, DEC_BATCH, PAST_LEN, FOX_HEADS, HEAD_DIM), 1.0)
    cache_fox_logf = jax.nn.log_sigmoid(3.0 + nrm((DEPTH, DEC_BATCH, PAST_LEN, FOX_HEADS), 1.0))
    state_ssm = nrm((DEPTH, DEC_BATCH, SSD_HEADS, SSD_HEAD_DIM, SSD_STATE), 0.1)
    state_conv = nrm((DEPTH, DEC_BATCH, SSD_CONV - 1, CONV_DIM), 1.0)
    cache_mem_k = nrm((DEPTH, DEC_BATCH, MEM_LEN, X_HEADS, X_HEAD_DIM), 1.0)
    cache_mem_v = nrm((DEPTH, DEC_BATCH, MEM_LEN, X_HEADS, X_HEAD_DIM), 1.0)
    mem_prompt = nrm((BATCH, MEM_LEN, D_MODEL), 1.0)
    norm_mix_g = gain((DEPTH, D_MODEL))
    w_in = nrm((DEPTH, D_MODEL, N_IN), D_MODEL ** -0.5)
    fox_b_f = jax.random.uniform(nk(), (DEPTH, FOX_HEADS), jnp.float32, 1.0, 6.0)
    conv_w = nrm((DEPTH, SSD_CONV, CONV_DIM), SSD_CONV ** -0.5)
    conv_b = nrm((DEPTH, CONV_DIM), 0.02)
    dt0 = jnp.exp(jax.random.uniform(nk(), (DEPTH, SSD_HEADS), jnp.float32, math.log(1e-3), math.log(1e-1)))
    dt_bias = dt0 + jnp.log(-jnp.expm1(-dt0))
    a_log = jnp.log(jax.random.uniform(nk(), (DEPTH, SSD_HEADS), jnp.float32, 1.0, 16.0))
    d_skip = gain((DEPTH, SSD_HEADS))
    ssd_norm_g = gain((DEPTH, SSD_INNER))
    w_out = nrm((DEPTH, D_MIX, D_MODEL), D_MIX ** -0.5)
    norm_x_g = gain((DEPTH, D_MODEL))
    mem_norm_g = gain((DEPTH, D_MODEL))
    wq_x = nrm((DEPTH, D_MODEL, D_MODEL), D_MODEL ** -0.5)
    wk_x = nrm((DEPTH, D_MODEL, D_MODEL), D_MODEL ** -0.5)
    wv_x = nrm((DEPTH, D_MODEL, D_MODEL), D_MODEL ** -0.5)
    wo_x = nrm((DEPTH, D_MODEL, D_MODEL), D_MODEL ** -0.5)
    norm_ffn_g = gain((DEPTH, D_MODEL))
    w_gate = nrm((DEPTH, D_MODEL, D_FF), D_MODEL ** -0.5)
    w_up = nrm((DEPTH, D_MODEL, D_FF), D_MODEL ** -0.5)
    w_down = nrm((DEPTH, D_FF, D_MODEL), D_FF ** -0.5)
    final_norm_g = gain((D_MODEL,))
    return {'x_prompt': x_prompt, 'x_sample': x_sample,
            'cache_sb_k': cache_sb_k, 'cache_sb_v': cache_sb_v,
            'cache_fox_k': cache_fox_k, 'cache_fox_v': cache_fox_v, 'cache_fox_logf': cache_fox_logf,
            'state_ssm': state_ssm, 'state_conv': state_conv,
            'cache_mem_k': cache_mem_k, 'cache_mem_v': cache_mem_v,
            'mem_prompt': mem_prompt,
            'norm_mix_g': norm_mix_g, 'w_in': w_in, 'fox_b_f': fox_b_f, 'conv_w': conv_w, 'conv_b': conv_b,
            'dt_bias': dt_bias, 'a_log': a_log, 'd_skip': d_skip, 'ssd_norm_g': ssd_norm_g, 'w_out': w_out,
            'norm_x_g': norm_x_g, 'mem_norm_g': mem_norm_g, 'wq_x': wq_x, 'wk_x': wk_x, 'wv_x': wv_x, 'wo_x': wo_x,
            'norm_ffn_g': norm_ffn_g, 'w_gate': w_gate, 'w_up': w_up, 'w_down': w_down,
            'final_norm_g': final_norm_g}


def reference(x_prompt, x_sample, cache_sb_k, cache_sb_v, cache_fox_k, cache_fox_v, cache_fox_logf,
              state_ssm, state_conv, cache_mem_k, cache_mem_v, mem_prompt,
              norm_mix_g, w_in, fox_b_f, conv_w, conv_b, dt_bias, a_log, d_skip, ssd_norm_g, w_out,
              norm_x_g, mem_norm_g, wq_x, wk_x, wv_x, wo_x, norm_ffn_g, w_gate, w_up, w_down, final_norm_g):
    hp, hs = x_prompt, x_sample
    p_st, s_st, p_mk, p_mv = [], [], [], []
    for l in range(DEPTH):
        p = dict(norm_mix_g=norm_mix_g[l], w_in=w_in[l], fox_b_f=fox_b_f[l], conv_w=conv_w[l], conv_b=conv_b[l],
                 dt_bias=dt_bias[l], a_log=a_log[l], d_skip=d_skip[l], ssd_norm_g=ssd_norm_g[l], w_out=w_out[l],
                 norm_x_g=norm_x_g[l], wq_x=wq_x[l], wo_x=wo_x[l], norm_ffn_g=norm_ffn_g[l],
                 w_gate=w_gate[l], w_up=w_up[l], w_down=w_down[l])
        mk, mv = mem_kv(mem_prompt, mem_norm_g[l], wk_x[l], wv_x[l])
        hp, st = layer(hp, p, None, mk, mv)
        p_st.append(st)
        p_mk.append(mk)
        p_mv.append(mv)
        past = dict(sb_k=cache_sb_k[l], sb_v=cache_sb_v[l], fox_k=cache_fox_k[l], fox_v=cache_fox_v[l],
                    fox_logf=cache_fox_logf[l], ssm=state_ssm[l], conv=state_conv[l])
        hs, st = layer(hs, p, past, cache_mem_k[l], cache_mem_v[l])
        s_st.append(st)
    y_prompt = rmsnorm(hp, final_norm_g)
    y_sample = rmsnorm(hs, final_norm_g)
    stk = lambda sts, i: jnp.stack([s[i] for s in sts], axis=0)
    p_sb_k, p_sb_v, p_fox_k, p_fox_v = stk(p_st, 0), stk(p_st, 1), stk(p_st, 2), stk(p_st, 3)
    p_fox_logf, p_ssm, p_conv = stk(p_st, 4), stk(p_st, 5), stk(p_st, 6)
    p_mem_k, p_mem_v = jnp.stack(p_mk, axis=0), jnp.stack(p_mv, axis=0)
    s_sb_k, s_sb_v, s_fox_k, s_fox_v = stk(s_st, 0), stk(s_st, 1), stk(s_st, 2), stk(s_st, 3)
    s_fox_logf, s_ssm, s_conv = stk(s_st, 4), stk(s_st, 5), stk(s_st, 6)
    return (y_prompt, y_sample,
            p_sb_k, p_sb_v, p_fox_k, p_fox_v, p_fox_logf, p_ssm, p_conv, p_mem_k, p_mem_v,
            s_sb_k, s_sb_v, s_fox_k, s_fox_v, s_fox_logf, s_ssm, s_conv)
```

```python
import functools

import jax
import jax.numpy as jnp
from jax import lax
from jax.experimental import pallas as pl
from jax.experimental.pallas import tpu as pltpu

F32 = jnp.float32
BF16 = jnp.bfloat16
EPS = 1e-6

HEAD_DIM = 64
ATT_HEADS = 4
ATT_WIDTH = ATT_HEADS * HEAD_DIM
SSD_HEADS = 8
SSD_HEAD_DIM = 64
SSD_INNER = SSD_HEADS * SSD_HEAD_DIM
SSD_GROUPS = 2
SSD_STATE = 128
SSD_CONV = 4
CONV_DIM = SSD_INNER + 2 * SSD_GROUPS * SSD_STATE
X_HEADS = 4
LANES = 128
SMALL_LOGF = 0
SMALL_DT = ATT_HEADS
CONV_PAD = 8
ATT_BLOCK = 256
NEG = -1e30
VMEM_LIMIT = 56 * 1024 * 1024

NT_DIMS = (((1,), (1,)), ((), ()))
TN_DIMS = (((0,), (0,)), ((), ()))


def _params(*sem):
    return pltpu.CompilerParams(dimension_semantics=sem, vmem_limit_bytes=VMEM_LIMIT)


def _rms(x, g):
    ms = jnp.mean(x * x, axis=-1, keepdims=True)
    return x * lax.rsqrt(ms + EPS) * g


def _softplus(x):
    return jnp.maximum(x, 0.0) + jnp.log1p(jnp.exp(-jnp.abs(x)))


def _silu(x):
    return x * (1.0 / (1.0 + jnp.exp(-x)))


def _dot(a, b):
    return jnp.dot(a, b, preferred_element_type=F32)


def _dot_nt(a, b):
    return lax.dot_general(a, b, NT_DIMS, preferred_element_type=F32)


def _split3(x):
    hi = x.astype(BF16)
    r1 = x - hi.astype(F32)
    mid = r1.astype(BF16)
    lo = (r1 - mid.astype(F32)).astype(BF16)
    return hi, mid, lo


def _const_spec(shape):
    return pl.BlockSpec(shape, lambda *_: (0,) * len(shape))


def _in_proj_kernel(x_ref, g_ref, wm_ref, ws_ref, bs_ref,
                    sq_ref, sk_ref, sv_ref, fq_ref, fk_ref, fv_ref, z_ref, xbc_ref, small_ref,
                    skb_ref, svb_ref, fkb_ref, fvb_ref):
    u = _rms(x_ref[...], g_ref[...]).astype(BF16)
    scale = HEAD_DIM ** -0.5
    off = 0

    def mm(n):
        nonlocal off
        y = _dot(u, wm_ref[:, off:off + n])
        off += n
        return y

    sq_ref[...] = (mm(ATT_WIDTH) * scale).astype(BF16)
    k = mm(ATT_WIDTH)
    sk_ref[...] = k
    skb_ref[...] = k.astype(BF16)
    v = mm(ATT_WIDTH)
    sv_ref[...] = v
    svb_ref[...] = v.astype(BF16)
    fq_ref[...] = (mm(ATT_WIDTH) * scale).astype(BF16)
    k = mm(ATT_WIDTH)
    fk_ref[...] = k
    fkb_ref[...] = k.astype(BF16)
    v = mm(ATT_WIDTH)
    fv_ref[...] = v
    fvb_ref[...] = v.astype(BF16)
    z_ref[...] = mm(SSD_INNER)
    xbc_ref[...] = mm(CONV_DIM)
    s = _dot(u, ws_ref[...]) + bs_ref[...]
    lane = lax.broadcasted_iota(jnp.int32, s.shape, 1)
    is_gate = lane < SMALL_DT
    sp = _softplus(jnp.where(is_gate, -s, s))
    small_ref[...] = jnp.where(is_gate, -sp, sp)


def _in_proj(x, g, w_main, w_small, b_small, tm):
    m, d = x.shape
    n_main = w_main.shape[1]
    row = lambda n: pl.BlockSpec((tm, n), lambda i: (i, 0))
    widths = [ATT_WIDTH] * 6 + [SSD_INNER, CONV_DIM, LANES] + [ATT_WIDTH] * 4
    dtypes = [BF16, F32, F32, BF16, F32, F32, F32, F32, F32, BF16, BF16, BF16, BF16]
    return pl.pallas_call(
        _in_proj_kernel,
        grid=(m // tm,),
        in_specs=[row(d), _const_spec((1, d)), _const_spec((d, n_main)), _const_spec((d, LANES)),
                  _const_spec((1, LANES))],
        out_specs=[row(n) for n in widths],
        out_shape=[jax.ShapeDtypeStruct((m, n), dt) for n, dt in zip(widths, dtypes)],
        compiler_params=_params("parallel"),
        name="in_proj",
    )(x, g, w_main, w_small, b_small)


def _cumsum_kernel(x_ref, o_ref, *, blk):
    n = x_ref.shape[2]
    r = lax.broadcasted_iota(jnp.int32, (blk, blk), 0)
    c = lax.broadcasted_iota(jnp.int32, (blk, blk), 1)
    tri = jnp.where(r <= c, 1.0, 0.0).astype(BF16)
    tri3 = jnp.concatenate([tri, tri, tri], axis=0)
    carry = jnp.zeros((x_ref.shape[1], 1), F32)
    for j in range(n // blk):
        xb = x_ref[0, :, j * blk:(j + 1) * blk]
        out = _dot(jnp.concatenate(_split3(xb), axis=1), tri3) + carry
        o_ref[0, :, j * blk:(j + 1) * blk] = out
        carry = out[:, blk - 1:blk]


def _seq_cumsum(x):
    b, r, n = x.shape
    spec = pl.BlockSpec((1, r, n), lambda i: (i, 0, 0))
    return pl.pallas_call(
        functools.partial(_cumsum_kernel, blk=ATT_BLOCK),
        grid=(b,),
        in_specs=[spec],
        out_specs=spec,
        out_shape=jax.ShapeDtypeStruct(x.shape, F32),
        compiler_params=_params("parallel"),
        name="seq_cumsum",
    )(x)


def _stack_heads(q, qbd_ref, tq):
    lane_head = lax.broadcasted_iota(jnp.int32, q.shape, 1) // HEAD_DIM
    for h in range(ATT_HEADS):
        qbd_ref[h * tq:(h + 1) * tq, :] = jnp.where(lane_head == h, q, jnp.zeros_like(q))


def _unstack_heads(acc, tq):
    lane_head = lax.broadcasted_iota(jnp.int32, (tq, ATT_WIDTH), 1) // HEAD_DIM
    out = jnp.zeros((tq, ATT_WIDTH), F32)
    for h in range(ATT_HEADS):
        out = jnp.where(lane_head == h, acc[h * tq:(h + 1) * tq, :], out)
    return out


def _local_positions(rows, tq, tk):
    t_loc = lax.broadcasted_iota(jnp.int32, (rows, tk), 0) & (tq - 1)
    s_loc = lax.broadcasted_iota(jnp.int32, (rows, tk), 1)
    return t_loc, s_loc


def _sb_kernel(q_ref, kd_ref, vd_ref, kp_ref, vp_ref, o_ref, qbd_ref, acc_ref, carry_ref,
               *, tq, tk, n_past):
    rows = ATT_HEADS * tq
    _stack_heads(q_ref[0], qbd_ref, tq)
    acc_ref[...] = jnp.zeros_like(acc_ref)
    carry_ref[...] = jnp.zeros_like(carry_ref)
    r = lax.broadcasted_iota(jnp.int32, (2 * LANES, 2 * LANES), 0) & (LANES - 1)
    c = lax.broadcasted_iota(jnp.int32, (2 * LANES, 2 * LANES), 1)
    wcum = jnp.where((c >= LANES) | (r > c), 1.0, 0.0).astype(BF16)

    def block(kb, vb, diagonal):
        s = _dot_nt(qbd_ref[...], kb)
        sp = _softplus(s)
        log_keep = -sp
        log_beta = s - sp
        if diagonal:
            t_loc, s_loc = _local_positions(rows, tq, tk)
            valid = s_loc < t_loc
            log_keep = jnp.where(valid, log_keep, 0.0)
        hi = log_keep.astype(BF16)
        lo = (log_keep - hi.astype(F32)).astype(BF16)
        carry = carry_ref[...]
        after = [None] * (tk // LANES)
        for g in reversed(range(tk // LANES)):
            cols = slice(g * LANES, (g + 1) * LANES)
            a = _dot(jnp.concatenate([hi[:, cols], lo[:, cols]], axis=1), wcum)
            after[g] = a[:, :LANES] + carry
            carry = carry + a[:, LANES:]
        carry_ref[...] = carry
        w = jnp.exp(log_beta + jnp.concatenate(after, axis=1))
        if diagonal:
            w = jnp.where(valid, w, 0.0)
        acc_ref[...] += _dot(w.astype(BF16), vb)

    block(kd_ref[0], vd_ref[0], True)
    n_blocks = pl.program_id(1) if n_past is None else n_past

    def body(step, _):
        start = pl.multiple_of((n_blocks - 1 - step) * tk, tk)
        block(kp_ref[0, pl.ds(start, tk), :].astype(BF16), vp_ref[0, pl.ds(start, tk), :].astype(BF16), False)
        return 0

    lax.fori_loop(0, n_blocks, body, 0)
    o_ref[0] = _unstack_heads(acc_ref[...], tq)


def _attn_specs(tq, tk, kd, kp, prompt):
    b, n_past_rows, _ = kp.shape
    q_spec = pl.BlockSpec((1, tq, ATT_WIDTH), lambda i, j: (i, j, 0))
    if prompt:
        diag_spec = pl.BlockSpec((1, tk, ATT_WIDTH), lambda i, j: (i, j, 0))
    else:
        diag_spec = pl.BlockSpec((1, tk, ATT_WIDTH), lambda i, j: (i, 0, 0))
    past_spec = pl.BlockSpec((1, n_past_rows, ATT_WIDTH), lambda i, j: (i, 0, 0))
    return q_spec, diag_spec, past_spec


def _sb_attn(q, kd, vd, kp, vp, prompt):
    b, n, _ = q.shape
    tk = ATT_BLOCK
    tq = ATT_BLOCK if prompt else n
    rows = ATT_HEADS * tq
    q_spec, diag_spec, past_spec = _attn_specs(tq, tk, kd, kp, prompt)
    n_past = None if prompt else kp.shape[1] // tk
    return pl.pallas_call(
        functools.partial(_sb_kernel, tq=tq, tk=tk, n_past=n_past),
        grid=(b, n // tq),
        in_specs=[q_spec, diag_spec, diag_spec, past_spec, past_spec],
        out_specs=q_spec,
        out_shape=jax.ShapeDtypeStruct((b, n, ATT_WIDTH), F32),
        scratch_shapes=[pltpu.VMEM((rows, ATT_WIDTH), BF16), pltpu.VMEM((rows, ATT_WIDTH), F32),
                        pltpu.VMEM((rows, LANES), F32)],
        compiler_params=_params("parallel", "arbitrary"),
        name="sb_attn",
    )(q, kd, vd, kp, vp)


def _fox_kernel(q_ref, kd_ref, vd_ref, fd_ref, kp_ref, vp_ref, fp_ref, o_ref,
                qbd_ref, acc_ref, m_ref, l_ref, *, tq, tk, n_past):
    rows = ATT_HEADS * tq
    _stack_heads(q_ref[0], qbd_ref, tq)
    acc_ref[...] = jnp.zeros_like(acc_ref)
    m_ref[...] = jnp.full_like(m_ref, NEG)
    l_ref[...] = jnp.zeros_like(l_ref)

    def block(kb, vb, fk, diagonal):
        s = _dot_nt(qbd_ref[...], kb)
        if diagonal:
            t_loc, s_loc = _local_positions(tq, tq, tk)
            valid = s_loc <= t_loc
        ps = []
        for h in range(ATT_HEADS):
            rs = slice(h * tq, (h + 1) * tq)
            sh = s[rs] - fk[h:h + 1, :]
            if diagonal:
                sh = jnp.where(valid, sh, NEG)
            m_old = m_ref[rs]
            m_new = jnp.maximum(m_old, jnp.max(sh, axis=1, keepdims=True))
            p = jnp.exp(sh - m_new)
            alpha = jnp.exp(m_old - m_new)
            l_ref[rs] = alpha * l_ref[rs] + jnp.sum(p, axis=1, keepdims=True)
            m_ref[rs] = m_new
            acc_ref[rs] = acc_ref[rs] * alpha
            ps.append(p.astype(BF16))
        acc_ref[...] += _dot(jnp.concatenate(ps, axis=0), vb)

    block(kd_ref[0], vd_ref[0], fd_ref[0], True)
    n_blocks = pl.program_id(1) if n_past is None else n_past

    def body(step, _):
        start = pl.multiple_of((n_blocks - 1 - step) * tk, tk)
        block(kp_ref[0, pl.ds(start, tk), :].astype(BF16), vp_ref[0, pl.ds(start, tk), :].astype(BF16),
              fp_ref[0, :, pl.ds(start, tk)], False)
        return 0

    lax.fori_loop(0, n_blocks, body, 0)
    o_ref[0] = _unstack_heads(acc_ref[...] * (1.0 / l_ref[...]), tq)


def _fox_attn(q, kd, vd, kp, vp, fcum, prompt):
    b, n, _ = q.shape
    tk = ATT_BLOCK
    tq = ATT_BLOCK if prompt else n
    rows = ATT_HEADS * tq
    q_spec, diag_spec, past_spec = _attn_specs(tq, tk, kd, kp, prompt)
    n_past = None if prompt else kp.shape[1] // tk
    fr = fcum.shape[1]
    if prompt:
        fd_spec = pl.BlockSpec((1, fr, tk), lambda i, j: (i, 0, j))
    else:
        fd_spec = pl.BlockSpec((1, fr, tk), lambda i, j: (i, 0, n_past))
    fp_spec = pl.BlockSpec((1, fr, fcum.shape[2]), lambda i, j: (i, 0, 0))
    return pl.pallas_call(
        functools.partial(_fox_kernel, tq=tq, tk=tk, n_past=n_past),
        grid=(b, n // tq),
        in_specs=[q_spec, diag_spec, diag_spec, fd_spec, past_spec, past_spec, fp_spec],
        out_specs=q_spec,
        out_shape=jax.ShapeDtypeStruct((b, n, ATT_WIDTH), F32),
        scratch_shapes=[pltpu.VMEM((rows, ATT_WIDTH), BF16), pltpu.VMEM((rows, ATT_WIDTH), F32),
                        pltpu.VMEM((rows, 1), F32), pltpu.VMEM((rows, 1), F32)],
        compiler_params=_params("parallel", "arbitrary"),
        name="fox_attn",
    )(q, kd, vd, fcum, kp, vp, fcum)


def _ssd_kernel(xbc_ref, z_ref, sm_ref, smt_ref, cinit_ref, h0_ref, cw_ref, cb_ref, alane_ref, acol_ref,
                dskip_ref, g_ref, y_ref, hfin_ref, xp_ref, h_ref, *, L):
    @pl.when(pl.program_id(1) == 0)
    def _():
        h_ref[...] = h0_ref[0]
        xp_ref[0:CONV_PAD, :] = cinit_ref[0]

    xp_ref[CONV_PAD:CONV_PAD + L, :] = xbc_ref[0]
    first = CONV_PAD - (SSD_CONV - 1)
    conv = cb_ref[...]
    for i in range(SSD_CONV):
        conv = conv + cw_ref[i:i + 1, :] * xp_ref[first + i:first + i + L, :]
    xp_ref[0:CONV_PAD, :] = xp_ref[L:L + CONV_PAD, :]
    xc = _silu(conv)
    xs = xc[:, :SSD_INNER]

    dt_col = sm_ref[0]
    a_col = dt_col * alane_ref[...]
    ri = lax.broadcasted_iota(jnp.int32, (L, L), 0)
    ci = lax.broadcasted_iota(jnp.int32, (L, L), 1)
    causal = ci <= ri
    tri_lower = jnp.where(causal, 1.0, 0.0).astype(BF16)
    tri_upper = jnp.where(ri <= ci, 1.0, 0.0).astype(BF16)
    acum_col = sum(_dot(tri_lower, part) for part in _split3(a_col))
    a_row = smt_ref[0] * acol_ref[...]
    acum_row = _dot(jnp.concatenate(_split3(a_row), axis=1),
                    jnp.concatenate([tri_upper] * 3, axis=0))
    a_last = acum_col[L - 1:L, :]

    ys = []
    heads_per_group = SSD_HEADS // SSD_GROUPS
    for grp in range(SSD_GROUPS):
        b0 = SSD_INNER + grp * SSD_STATE
        c0 = SSD_INNER + SSD_GROUPS * SSD_STATE + grp * SSD_STATE
        bg = xc[:, b0:b0 + SSD_STATE].astype(BF16)
        cg = xc[:, c0:c0 + SSD_STATE].astype(BF16)
        cb = _dot_nt(cg, bg)
        for hh in range(heads_per_group):
            h = grp * heads_per_group + hh
            col = SMALL_DT + h
            ac = acum_col[:, col:col + 1]
            ar = acum_row[col:col + 1, :]
            decay = jnp.exp(jnp.where(causal, ac - ar, NEG))
            scores = (cb * decay).astype(BF16)
            xdt = xs[:, h * SSD_HEAD_DIM:(h + 1) * SSD_HEAD_DIM] * dt_col[:, col:col + 1]
            state = h_ref[h]
            y_h = _dot(scores, xdt.astype(BF16)) + _dot_nt(cg, state.astype(BF16)) * jnp.exp(ac)
            al = a_last[:, col:col + 1]
            xw = (xdt * jnp.exp(al - ac)).astype(BF16)
            h_ref[h] = jnp.exp(al) * state + lax.dot_general(xw, bg, TN_DIMS, preferred_element_type=F32)
            ys.append(y_h)
    y = jnp.concatenate(ys, axis=1) + dskip_ref[...] * xs
    y_ref[0] = _rms(y * _silu(z_ref[0]), g_ref[...])
    hfin_ref[0] = h_ref[...]


def _ssd(xbc, z, small, small_t, conv_init, h0, conv_w, conv_b, a_lane, a_col, dskip, g, L):
    b, n, _ = xbc.shape
    seq = lambda w: pl.BlockSpec((1, L, w), lambda i, j: (i, j, 0))
    per_b = lambda *s: pl.BlockSpec((1,) + s, lambda i, j: (i,) + (0,) * len(s))
    return pl.pallas_call(
        functools.partial(_ssd_kernel, L=L),
        grid=(b, n // L),
        in_specs=[seq(CONV_DIM), seq(SSD_INNER), seq(LANES),
                  pl.BlockSpec((1, small_t.shape[1], L), lambda i, j: (i, 0, j)),
                  per_b(CONV_PAD, CONV_DIM), per_b(SSD_HEADS, SSD_HEAD_DIM, SSD_STATE),
                  _const_spec(conv_w.shape), _const_spec(conv_b.shape), _const_spec(a_lane.shape),
                  _const_spec(a_col.shape), _const_spec(dskip.shape), _const_spec(g.shape)],
        out_specs=[seq(SSD_INNER), per_b(SSD_HEADS, SSD_HEAD_DIM, SSD_STATE)],
        out_shape=[jax.ShapeDtypeStruct((b, n, SSD_INNER), F32),
                   jax.ShapeDtypeStruct((b, SSD_HEADS, SSD_HEAD_DIM, SSD_STATE), F32)],
        scratch_shapes=[pltpu.VMEM((L + CONV_PAD, CONV_DIM), F32),
                        pltpu.VMEM((SSD_HEADS, SSD_HEAD_DIM, SSD_STATE), F32)],
        compiler_params=_params("parallel", "arbitrary"),
        name="ssd",
    )(xbc, z, small, small_t, conv_init, h0, conv_w, conv_b, a_lane, a_col, dskip, g)


def _post_mix_kernel(h_ref, sb_ref, fx_ref, ssd_ref, wout_ref, gx_ref, wq_ref, mk_ref, mv_ref, wo_ref, o_ref):
    d = h_ref.shape[2]
    xd = d // X_HEADS
    mix = (_dot(sb_ref[0].astype(BF16), wout_ref[0:ATT_WIDTH, :])
           + _dot(fx_ref[0].astype(BF16), wout_ref[ATT_WIDTH:2 * ATT_WIDTH, :])
           + _dot(ssd_ref[0].astype(BF16), wout_ref[2 * ATT_WIDTH:, :]))
    h1 = h_ref[0] + mix
    u = _rms(h1, gx_ref[...]).astype(BF16)
    q = (_dot(u, wq_ref[...]) * (xd ** -0.5)).astype(BF16)
    outs = []
    for hd in range(X_HEADS):
        cols = slice(hd * xd, (hd + 1) * xd)
        s = _dot_nt(q[:, cols], mk_ref[0, :, cols].astype(BF16))
        p = jnp.exp(s - jnp.max(s, axis=1, keepdims=True))
        o = _dot(p.astype(BF16), mv_ref[0, :, cols].astype(BF16))
        outs.append((o * (1.0 / jnp.sum(p, axis=1, keepdims=True))).astype(BF16))
    o_ref[0] = h1 + _dot(jnp.concatenate(outs, axis=1), wo_ref[...])


def _post_mix(h, sb_o, fox_o, ssd_o, w_out, gx, wq, mk, mv, wo, tm):
    b, n, d = h.shape
    seq = lambda w: pl.BlockSpec((1, tm, w), lambda i, j: (i, j, 0))
    mem_spec = pl.BlockSpec((1,) + mk.shape[1:], lambda i, j: (i, 0, 0))
    return pl.pallas_call(
        _post_mix_kernel,
        grid=(b, n // tm),
        in_specs=[seq(d), seq(ATT_WIDTH), seq(ATT_WIDTH), seq(SSD_INNER), _const_spec(w_out.shape),
                  _const_spec(gx.shape), _const_spec(wq.shape), mem_spec, mem_spec, _const_spec(wo.shape)],
        out_specs=seq(d),
        out_shape=jax.ShapeDtypeStruct(h.shape, F32),
        compiler_params=_params("parallel", "parallel"),
        name="post_mix",
    )(h, sb_o, fox_o, ssd_o, w_out, gx, wq, mk, mv, wo)


def _ffn_kernel(h_ref, g_ref, wg_ref, wu_ref, wd_ref, gf_ref, o_ref, *, final_norm):
    h = h_ref[...]
    u = _rms(h, g_ref[...]).astype(BF16)
    hidden = (_silu(_dot(u, wg_ref[...])) * _dot(u, wu_ref[...])).astype(BF16)
    out = h + _dot(hidden, wd_ref[...])
    if final_norm:
        out = _rms(out, gf_ref[...])
    o_ref[...] = out


def _ffn(h, g, wg, wu, wd, g_final, final_norm, tm):
    m, d = h.shape
    row = pl.BlockSpec((tm, d), lambda i: (i, 0))
    return pl.pallas_call(
        functools.partial(_ffn_kernel, final_norm=final_norm),
        grid=(m // tm,),
        in_specs=[row, _const_spec(g.shape), _const_spec(wg.shape), _const_spec(wu.shape),
                  _const_spec(wd.shape), _const_spec(g_final.shape)],
        out_specs=row,
        out_shape=jax.ShapeDtypeStruct(h.shape, F32),
        compiler_params=_params("parallel"),
        name="ffn",
    )(h, g, wg, wu, wd, g_final)


def _mem_kv_kernel(x_ref, g_ref, wk_ref, wv_ref, k_ref, v_ref):
    u = _rms(x_ref[...], g_ref[...]).astype(BF16)
    k_ref[...] = _dot(u, wk_ref[...])
    v_ref[...] = _dot(u, wv_ref[...])


def _mem_kv(mem, g, wk, wv, tm):
    m, d = mem.shape
    row = pl.BlockSpec((tm, d), lambda i: (i, 0))
    return pl.pallas_call(
        _mem_kv_kernel,
        grid=(m // tm,),
        in_specs=[row, _const_spec(g.shape), _const_spec(wk.shape), _const_spec(wv.shape)],
        out_specs=[row, row],
        out_shape=[jax.ShapeDtypeStruct((m, d), F32)] * 2,
        compiler_params=_params("parallel"),
        name="mem_kv",
    )(mem, g, wk, wv)


def _layer(h, p, past, mem_k, mem_v, final_g, final_norm):
    b, n, d = h.shape
    m = b * n
    prompt = past is None
    tm = min(m, 512)
    (sq, sk, sv, fq, fk, fv, z, xbc, small, skb, svb, fkb, fvb) = _in_proj(
        h.reshape(m, d), p["norm_mix_g"], p["w_main"], p["w_small"], p["b_small"], tm)
    seq3 = lambda a: a.reshape(b, n, a.shape[-1])
    sq, fq, z, xbc, small = map(seq3, (sq, fq, z, xbc, small))
    skb, svb, fkb, fvb = map(seq3, (skb, svb, fkb, fvb))
    logf = small[:, :, SMALL_LOGF:SMALL_LOGF + ATT_HEADS]
    small_t = jnp.transpose(small[:, :, :2 * CONV_PAD], (0, 2, 1))

    if prompt:
        fcum = _seq_cumsum(small_t[:, :CONV_PAD, :])
        sb_o = _sb_attn(sq, skb, svb, skb, svb, True)
        fox_o = _fox_attn(fq, fkb, fvb, fkb, fvb, fcum, True)
        conv_init = jnp.zeros((b, CONV_PAD, CONV_DIM), F32)
        h0 = jnp.zeros((b, SSD_HEADS, SSD_HEAD_DIM, SSD_STATE), F32)
        conv_hist = None
        ssd_len = ATT_BLOCK
    else:
        past_len = past["sb_k"].shape[1]
        pad_rows = lambda a: jnp.pad(a, ((0, 0), (0, ATT_BLOCK - n), (0, 0)))
        flat = lambda a: a.reshape(b, past_len, ATT_WIDTH)
        sb_o = _sb_attn(sq, pad_rows(skb), pad_rows(svb), flat(past["sb_k"]), flat(past["sb_v"]), False)
        logf_all = jnp.concatenate([jnp.transpose(past["fox_logf"], (0, 2, 1)),
                                    jnp.transpose(logf, (0, 2, 1))], axis=2)
        logf_all = jnp.pad(logf_all, ((0, 0), (0, CONV_PAD - ATT_HEADS), (0, ATT_BLOCK - n)))
        fcum = _seq_cumsum(logf_all)
        fox_o = _fox_attn(fq, pad_rows(fkb), pad_rows(fvb), flat(past["fox_k"]), flat(past["fox_v"]), fcum, False)
        conv_hist = past["conv"]
        conv_init = jnp.pad(conv_hist, ((0, 0), (CONV_PAD - (SSD_CONV - 1), 0), (0, 0)))
        h0 = past["ssm"]
        ssd_len = n

    ssd_o, h_final = _ssd(xbc, z, small, small_t, conv_init, h0, p["conv_w"], p["conv_b"],
                          p["a_lane"], p["a_col"], p["dskip_vec"], p["ssd_norm_g"], ssd_len)
    if conv_hist is None:
        conv_state = xbc[:, n - (SSD_CONV - 1):, :]
    else:
        conv_state = jnp.concatenate([conv_hist, xbc], axis=1)[:, -(SSD_CONV - 1):, :]

    h1 = _post_mix(h, sb_o, fox_o, ssd_o, p["w_out"], p["norm_x_g"], p["wq_x"], mem_k, mem_v, p["wo_x"],
                   min(n, 512))
    h2 = _ffn(h1.reshape(m, d), p["norm_ffn_g"], p["w_gate"], p["w_up"], p["w_down"], final_g, final_norm, tm)
    heads = lambda a: a.reshape(b, n, ATT_HEADS, HEAD_DIM)
    return h2.reshape(b, n, d), (heads(sk), heads(sv), heads(fk), heads(fv), logf, h_final, conv_state)


def _layer_params(l, norm_mix_g, w_in, fox_b_f, conv_w, conv_b, dt_bias, a_log, d_skip, ssd_norm_g, w_out,
                  norm_x_g, wq_x, wo_x, norm_ffn_g, w_gate, w_up, w_down):
    qkv = 6 * ATT_WIDTH
    gate_end = qkv + ATT_HEADS
    z_end = gate_end + SSD_INNER
    xbc_end = z_end + CONV_DIM
    w = w_in[l]
    d = w.shape[0]
    row = lambda a: a.reshape(1, -1).astype(F32)
    n_small = ATT_HEADS + SSD_HEADS
    small_pad = LANES - n_small
    a_neg = -jnp.exp(a_log[l].astype(F32))
    a_small = jnp.concatenate([jnp.zeros((ATT_HEADS,), F32), a_neg, jnp.zeros((small_pad,), F32)])
    return dict(
        norm_mix_g=row(norm_mix_g[l]),
        w_main=jnp.concatenate([w[:, :qkv], w[:, gate_end:xbc_end]], axis=1).astype(BF16),
        w_small=jnp.concatenate([w[:, qkv:gate_end], w[:, xbc_end:], jnp.zeros((d, small_pad), w.dtype)],
                                axis=1).astype(BF16),
        b_small=jnp.concatenate([fox_b_f[l], dt_bias[l], jnp.zeros((small_pad,), F32)]).reshape(1, LANES),
        conv_w=conv_w[l].astype(F32), conv_b=row(conv_b[l]),
        a_lane=a_small.reshape(1, LANES), a_col=a_small[:2 * CONV_PAD].reshape(2 * CONV_PAD, 1),
        dskip_vec=row(jnp.repeat(d_skip[l], SSD_HEAD_DIM)), ssd_norm_g=row(ssd_norm_g[l]),
        w_out=w_out[l].astype(BF16), norm_x_g=row(norm_x_g[l]),
        wq_x=wq_x[l].astype(BF16), wo_x=wo_x[l].astype(BF16), norm_ffn_g=row(norm_ffn_g[l]),
        w_gate=w_gate[l].astype(BF16), w_up=w_up[l].astype(BF16), w_down=w_down[l].astype(BF16))


def kernel(x_prompt, x_sample, cache_sb_k, cache_sb_v, cache_fox_k, cache_fox_v, cache_fox_logf, state_ssm, state_conv, cache_mem_k, cache_mem_v, mem_prompt, norm_mix_g, w_in, fox_b_f, conv_w, conv_b, dt_bias, a_log, d_skip, ssd_norm_g, w_out, norm_x_g, mem_norm_g, wq_x, wk_x, wv_x, wo_x, norm_ffn_g, w_gate, w_up, w_down, final_norm_g):
    depth = w_in.shape[0]
    bp, mem_len, d = mem_prompt.shape
    bs = x_sample.shape[0]
    final_g = final_norm_g.reshape(1, d).astype(F32)
    hp, hs = x_prompt, x_sample
    p_st, s_st, p_mk, p_mv = [], [], [], []
    for l in range(depth):
        p = _layer_params(l, norm_mix_g, w_in, fox_b_f, conv_w, conv_b, dt_bias, a_log, d_skip, ssd_norm_g,
                          w_out, norm_x_g, wq_x, wo_x, norm_ffn_g, w_gate, w_up, w_down)
        last = l == depth - 1
        mk, mv = _mem_kv(mem_prompt.reshape(bp * mem_len, d), mem_norm_g[l].reshape(1, d).astype(F32),
                         wk_x[l].astype(BF16), wv_x[l].astype(BF16), 512)
        mk = mk.reshape(bp, mem_len, d)
        mv = mv.reshape(bp, mem_len, d)
        hp, st = _layer(hp, p, None, mk, mv, final_g, last)
        p_st.append(st)
        p_mk.append(mk.reshape(bp, mem_len, X_HEADS, d // X_HEADS))
        p_mv.append(mv.reshape(bp, mem_len, X_HEADS, d // X_HEADS))
        past = dict(sb_k=cache_sb_k[l], sb_v=cache_sb_v[l], fox_k=cache_fox_k[l], fox_v=cache_fox_v[l],
                    fox_logf=cache_fox_logf[l], ssm=state_ssm[l], conv=state_conv[l])
        hs, st = _layer(hs, p, past, cache_mem_k[l].reshape(bs, mem_len, d),
                        cache_mem_v[l].reshape(bs, mem_len, d), final_g, last)
        s_st.append(st)
    stk = lambda sts, i: jnp.stack([s[i] for s in sts], axis=0)
    return (hp, hs,
            stk(p_st, 0), stk(p_st, 1), stk(p_st, 2), stk(p_st, 3), stk(p_st, 4), stk(p_st, 5), stk(p_st, 6),
            jnp.stack(p_mk, axis=0), jnp.stack(p_mv, axis=0),
            stk(s_st, 0), stk(s_st, 1), stk(s_st, 2), stk(s_st, 3), stk(s_st, 4), stk(s_st, 5), stk(s_st, 6))
```

```python
import functools

import jax
import jax.numpy as jnp
from jax import lax
from jax.experimental import pallas as pl
from jax.experimental.pallas import tpu as pltpu

F32 = jnp.float32
BF16 = jnp.bfloat16
EPS = 1e-6

HEAD_DIM = 64
ATT_HEADS = 4
ATT_WIDTH = ATT_HEADS * HEAD_DIM
SSD_HEADS = 8
SSD_HEAD_DIM = 64
SSD_INNER = SSD_HEADS * SSD_HEAD_DIM
SSD_GROUPS = 2
SSD_STATE = 128
SSD_CONV = 4
CONV_DIM = SSD_INNER + 2 * SSD_GROUPS * SSD_STATE
X_HEADS = 4
LANES = 128
SMALL_LOGF = 0
SMALL_DT = ATT_HEADS
CONV_PAD = 8
ATT_BLOCK = 256
SAMPLE_PAST_BLOCK = 1024
NEG = -1e30
LOG2E = 1.4426950408889634
VMEM_LIMIT = 56 * 1024 * 1024

NT_DIMS = (((1,), (1,)), ((), ()))
TN_DIMS = (((0,), (0,)), ((), ()))


def _params(*sem):
    return pltpu.CompilerParams(dimension_semantics=sem, vmem_limit_bytes=VMEM_LIMIT)


def _rms(x, g):
    ms = jnp.mean(x * x, axis=-1, keepdims=True)
    return x * lax.rsqrt(ms + EPS) * g


def _softplus(x):
    t = jnp.exp2(jnp.abs(x) * (-LOG2E))
    return jnp.maximum(x, 0.0) + jnp.log(1.0 + t)


def _silu(x):
    return x * (1.0 / (1.0 + jnp.exp(-x)))


def _dot(a, b):
    return jnp.dot(a, b, preferred_element_type=F32)


def _dot_nt(a, b):
    return lax.dot_general(a, b, NT_DIMS, preferred_element_type=F32)


def _split3(x):
    hi = x.astype(BF16)
    r1 = x - hi.astype(F32)
    mid = r1.astype(BF16)
    lo = (r1 - mid.astype(F32)).astype(BF16)
    return hi, mid, lo


def _const_spec(shape):
    return pl.BlockSpec(shape, lambda *_: (0,) * len(shape))


def _in_proj_kernel(x_ref, g_ref, wq_ref, wkv_ref, wzx_ref, ws_ref, bs_row_ref, bs_col_ref,
                    sq_ref, fq_ref, z_ref, xbc_ref, small_ref, small_t_ref,
                    skt_ref, svt_ref, fkt_ref, fvt_ref, skb_ref, svb_ref, fkb_ref, fvb_ref):
    u = _rms(x_ref[...], g_ref[...]).astype(BF16)
    q = _dot_nt(u, wq_ref[...]) * (HEAD_DIM ** -0.5)
    sq_ref[...] = q[:, :ATT_WIDTH].astype(BF16)
    fq_ref[...] = q[:, ATT_WIDTH:].astype(BF16)
    zx = _dot_nt(u, wzx_ref[...])
    z_ref[...] = zx[:, :SSD_INNER]
    xbc_ref[...] = zx[:, SSD_INNER:]
    kvt = _dot_nt(wkv_ref[...], u)
    for i, (f_ref, b_ref) in enumerate(((skt_ref, skb_ref), (svt_ref, svb_ref),
                                        (fkt_ref, fkb_ref), (fvt_ref, fvb_ref))):
        part = kvt[i * ATT_WIDTH:(i + 1) * ATT_WIDTH, :]
        f_ref[...] = part.reshape(f_ref.shape)
        b_ref[...] = part.astype(BF16).reshape(b_ref.shape)

    def gates(s, index):
        is_gate = index < SMALL_DT
        sp = _softplus(jnp.where(is_gate, -s, s))
        return jnp.where(is_gate, -sp, sp)

    s = _dot_nt(u, ws_ref[...]) + bs_row_ref[...]
    small_ref[...] = gates(s, lax.broadcasted_iota(jnp.int32, s.shape, 1))
    st = _dot_nt(ws_ref[0:2 * CONV_PAD, :], u) + bs_col_ref[...]
    small_t_ref[...] = gates(st, lax.broadcasted_iota(jnp.int32, st.shape, 0)).reshape(small_t_ref.shape)


def _in_proj(x, g, p, tm, batch):
    m, d = x.shape
    row = lambda n: pl.BlockSpec((tm, n), lambda i: (i, 0))
    if batch is None:
        t_shape = lambda r: (r, m)
        t_spec = lambda r: pl.BlockSpec((r, tm), lambda i: (0, i))
    else:
        nb = m // batch // tm
        t_shape = lambda r: (batch, r, m // batch)
        t_spec = lambda r: pl.BlockSpec((1, r, tm), lambda i: (i // nb, 0, i % nb))
    rows = [(ATT_WIDTH, BF16), (ATT_WIDTH, BF16), (SSD_INNER, F32), (CONV_DIM, F32), (LANES, F32)]
    cols = [(2 * CONV_PAD, F32)] + [(ATT_WIDTH, F32)] * 4 + [(ATT_WIDTH, BF16)] * 4
    weights = [p["wq_t"], p["wkv_t"], p["wzx_t"], p["ws_t"], p["b_small_row"], p["b_small_col"]]
    return pl.pallas_call(
        _in_proj_kernel,
        grid=(m // tm,),
        in_specs=[row(d), _const_spec((1, d))] + [_const_spec(w.shape) for w in weights],
        out_specs=[row(n) for n, _ in rows] + [t_spec(r) for r, _ in cols],
        out_shape=[jax.ShapeDtypeStruct((m, n), dt) for n, dt in rows]
        + [jax.ShapeDtypeStruct(t_shape(r), dt) for r, dt in cols],
        compiler_params=_params("parallel"),
        name="in_proj",
    )(x, g, *weights)


def _cumsum_kernel(x_ref, o_ref, *, blk):
    n = x_ref.shape[2]
    r = lax.broadcasted_iota(jnp.int32, (blk, blk), 0)
    c = lax.broadcasted_iota(jnp.int32, (blk, blk), 1)
    tri = jnp.where(r <= c, 1.0, 0.0).astype(BF16)
    tri3 = jnp.concatenate([tri, tri, tri], axis=0)
    carry = jnp.zeros((x_ref.shape[1], 1), F32)
    for j in range(n // blk):
        xb = x_ref[0, :, j * blk:(j + 1) * blk]
        out = _dot(jnp.concatenate(_split3(xb), axis=1), tri3) + carry
        o_ref[0, :, j * blk:(j + 1) * blk] = out
        carry = out[:, blk - 1:blk]


def _seq_cumsum(x):
    b, r, n = x.shape
    spec = pl.BlockSpec((1, r, n), lambda i: (i, 0, 0))
    return pl.pallas_call(
        functools.partial(_cumsum_kernel, blk=ATT_BLOCK),
        grid=(b,),
        in_specs=[spec],
        out_specs=spec,
        out_shape=jax.ShapeDtypeStruct(x.shape, F32),
        compiler_params=_params("parallel"),
        name="seq_cumsum",
    )(x)


def _stack_heads(q, qbd_ref, tq):
    lane_head = lax.broadcasted_iota(jnp.int32, q.shape, 1) // HEAD_DIM
    for h in range(ATT_HEADS):
        qbd_ref[h * tq:(h + 1) * tq, :] = jnp.where(lane_head == h, q, jnp.zeros_like(q))


def _unstack_heads(acc, tq):
    lane_head = lax.broadcasted_iota(jnp.int32, (tq, ATT_WIDTH), 1) // HEAD_DIM
    out = jnp.zeros((tq, ATT_WIDTH), F32)
    for h in range(ATT_HEADS):
        out = jnp.where(lane_head == h, acc[h * tq:(h + 1) * tq, :], out)
    return out


def _local_positions(rows, tq, tk):
    t_loc = lax.broadcasted_iota(jnp.int32, (rows, tk), 0) & (tq - 1)
    s_loc = lax.broadcasted_iota(jnp.int32, (rows, tk), 1)
    return t_loc, s_loc


def _lane_tile(x, width):
    return jnp.concatenate([x] * (width // LANES), axis=1)


def _past_loop(block, n_blocks, tp, refs):
    def body(step, _):
        start = pl.multiple_of((n_blocks - 1 - step) * tp, tp)
        block(*(r[0, :, pl.ds(start, tp)] for r in refs))
        return 0

    lax.fori_loop(0, n_blocks, body, 0)


def _attn_specs(tq, td, kd, kp, prompt):
    q_spec = pl.BlockSpec((1, tq, ATT_WIDTH), lambda i, j: (i, j, 0))
    if prompt:
        edge = lambda r: pl.BlockSpec((1, r, td), lambda i, j: (i, 0, j))
    else:
        edge = lambda r: pl.BlockSpec((1, r, td), lambda i, j: (i, 0, 0))
    past = lambda a: pl.BlockSpec((1,) + a.shape[1:], lambda i, j: (i, 0, 0))
    return q_spec, edge, past


def _sb_kernel(q_ref, kd_ref, vd_ref, kp_ref, vp_ref, o_ref, qbd_ref, acc_ref, carry_ref,
               *, tq, tp, n_past):
    rows = ATT_HEADS * tq
    _stack_heads(q_ref[0], qbd_ref, tq)
    acc_ref[...] = jnp.zeros_like(acc_ref)
    carry_ref[...] = jnp.zeros_like(carry_ref)
    r = lax.broadcasted_iota(jnp.int32, (2 * LANES, 2 * LANES), 0) & (LANES - 1)
    c = lax.broadcasted_iota(jnp.int32, (2 * LANES, 2 * LANES), 1)
    wcum = jnp.where((c >= LANES) | (r > c), -1.0, 0.0).astype(BF16)

    def block(kt, vt, diagonal=False):
        kt = kt.astype(BF16)
        vt = vt.astype(BF16)
        tk = kt.shape[1]
        s = _dot(qbd_ref[...], kt)
        sp = _softplus(s)
        log_beta = s - sp
        if diagonal:
            t_loc, s_loc = _local_positions(rows, tq, tk)
            valid = s_loc < t_loc
            sp = jnp.where(valid, sp, 0.0)
        hi = sp.astype(BF16)
        lo = (sp - hi.astype(F32)).astype(BF16)
        carry = carry_ref[...]
        after = [None] * (tk // LANES)
        for g in reversed(range(tk // LANES)):
            cols = slice(g * LANES, (g + 1) * LANES)
            a = _dot(jnp.concatenate([hi[:, cols], lo[:, cols]], axis=1), wcum)
            after[g] = a[:, :LANES] + carry
            carry = carry + a[:, LANES:]
        carry_ref[...] = carry
        w = jnp.exp(log_beta + jnp.concatenate(after, axis=1))
        if diagonal:
            w = jnp.where(valid, w, 0.0)
        acc_ref[...] += _dot_nt(w.astype(BF16), vt)

    block(kd_ref[0], vd_ref[0], diagonal=True)
    n_blocks = pl.program_id(1) if n_past is None else n_past
    _past_loop(block, n_blocks, tp, (kp_ref, vp_ref))
    o_ref[0] = _unstack_heads(acc_ref[...], tq)


def _sb_attn(q, kd, vd, kp, vp, prompt):
    b, n, _ = q.shape
    td = ATT_BLOCK
    tq = ATT_BLOCK if prompt else n
    tp = ATT_BLOCK if prompt else SAMPLE_PAST_BLOCK
    rows = ATT_HEADS * tq
    q_spec, edge, past = _attn_specs(tq, td, kd, kp, prompt)
    n_past = None if prompt else kp.shape[2] // tp
    return pl.pallas_call(
        functools.partial(_sb_kernel, tq=tq, tp=tp, n_past=n_past),
        grid=(b, n // tq),
        in_specs=[q_spec, edge(ATT_WIDTH), edge(ATT_WIDTH), past(kp), past(vp)],
        out_specs=q_spec,
        out_shape=jax.ShapeDtypeStruct((b, n, ATT_WIDTH), F32),
        scratch_shapes=[pltpu.VMEM((rows, ATT_WIDTH), BF16), pltpu.VMEM((rows, ATT_WIDTH), F32),
                        pltpu.VMEM((rows, LANES), F32)],
        compiler_params=_params("parallel", "arbitrary"),
        name="sb_attn",
    )(q, kd, vd, kp, vp)


def _fox_kernel(q_ref, kd_ref, vd_ref, fd_ref, kp_ref, vp_ref, fp_ref, o_ref,
                qbd_ref, acc_ref, m_ref, l_ref, *, tq, tp, n_past):
    rows = ATT_HEADS * tq
    _stack_heads(q_ref[0], qbd_ref, tq)
    acc_ref[...] = jnp.zeros_like(acc_ref)
    m_ref[...] = jnp.full_like(m_ref, NEG)
    l_ref[...] = jnp.zeros_like(l_ref)

    def block(kt, vt, fk, diagonal=False):
        kt = kt.astype(BF16)
        vt = vt.astype(BF16)
        tk = kt.shape[1]
        s = _dot(qbd_ref[...], kt)
        sh = jnp.concatenate([s[h * tq:(h + 1) * tq] - fk[h:h + 1, :] for h in range(ATT_HEADS)], axis=0)
        if diagonal:
            t_loc, s_loc = _local_positions(rows, tq, tk)
            sh = jnp.where(s_loc <= t_loc, sh, NEG)
        m_old = m_ref[...]
        m_new = jnp.maximum(m_old, jnp.max(sh, axis=1, keepdims=True))
        alpha = jnp.exp(m_old - m_new)
        p = jnp.exp(sh - _lane_tile(m_new, tk))
        l_ref[...] = alpha * l_ref[...] + jnp.sum(p, axis=1, keepdims=True)
        m_ref[...] = m_new
        acc_ref[...] = acc_ref[...] * _lane_tile(alpha, ATT_WIDTH) + _dot_nt(p.astype(BF16), vt)

    block(kd_ref[0], vd_ref[0], fd_ref[0], diagonal=True)
    n_blocks = pl.program_id(1) if n_past is None else n_past
    _past_loop(block, n_blocks, tp, (kp_ref, vp_ref, fp_ref))
    o_ref[0] = _unstack_heads(acc_ref[...] * _lane_tile(1.0 / l_ref[...], ATT_WIDTH), tq)


def _fox_attn(q, kd, vd, kp, vp, fcum, prompt):
    b, n, _ = q.shape
    td = ATT_BLOCK
    tq = ATT_BLOCK if prompt else n
    tp = ATT_BLOCK if prompt else SAMPLE_PAST_BLOCK
    rows = ATT_HEADS * tq
    q_spec, edge, past = _attn_specs(tq, td, kd, kp, prompt)
    n_past = None if prompt else kp.shape[2] // tp
    fr = fcum.shape[1]
    if prompt:
        fd_spec = edge(fr)
    else:
        last = kp.shape[2] // td
        fd_spec = pl.BlockSpec((1, fr, td), lambda i, j: (i, 0, last))
    return pl.pallas_call(
        functools.partial(_fox_kernel, tq=tq, tp=tp, n_past=n_past),
        grid=(b, n // tq),
        in_specs=[q_spec, edge(ATT_WIDTH), edge(ATT_WIDTH), fd_spec, past(kp), past(vp), past(fcum)],
        out_specs=q_spec,
        out_shape=jax.ShapeDtypeStruct((b, n, ATT_WIDTH), F32),
        scratch_shapes=[pltpu.VMEM((rows, ATT_WIDTH), BF16), pltpu.VMEM((rows, ATT_WIDTH), F32),
                        pltpu.VMEM((rows, LANES), F32), pltpu.VMEM((rows, LANES), F32)],
        compiler_params=_params("parallel", "arbitrary"),
        name="fox_attn",
    )(q, kd, vd, fcum, kp, vp, fcum)


def _ssd_kernel(xbc_ref, z_ref, sm_ref, smt_ref, cinit_ref, h0_ref, cw_ref, cb_ref, alane_ref, acol_ref,
                dskip_ref, g_ref, y_ref, hfin_ref, xp_ref, h_ref, *, L):
    @pl.when(pl.program_id(1) == 0)
    def _():
        h_ref[...] = h0_ref[0]
        xp_ref[0:CONV_PAD, :] = cinit_ref[0]

    xp_ref[CONV_PAD:CONV_PAD + L, :] = xbc_ref[0]
    first = CONV_PAD - (SSD_CONV - 1)
    conv = cb_ref[...]
    for i in range(SSD_CONV):
        conv = conv + cw_ref[i:i + 1, :] * xp_ref[first + i:first + i + L, :]
    xp_ref[0:CONV_PAD, :] = xp_ref[L:L + CONV_PAD, :]
    xc = _silu(conv)
    xs = xc[:, :SSD_INNER]

    dt_col = sm_ref[0]
    a_col = dt_col * alane_ref[...]
    ri = lax.broadcasted_iota(jnp.int32, (L, L), 0)
    ci = lax.broadcasted_iota(jnp.int32, (L, L), 1)
    causal = ci <= ri
    tri_lower = jnp.where(causal, 1.0, 0.0).astype(BF16)
    tri_upper = jnp.where(ri <= ci, 1.0, 0.0).astype(BF16)
    acum_col = sum(_dot(tri_lower, part) for part in _split3(a_col))
    a_row = smt_ref[0] * acol_ref[...]
    acum_row = _dot(jnp.concatenate(_split3(a_row), axis=1),
                    jnp.concatenate([tri_upper] * 3, axis=0))
    a_last = acum_col[L - 1:L, :]

    ys = []
    heads_per_group = SSD_HEADS // SSD_GROUPS
    for grp in range(SSD_GROUPS):
        b0 = SSD_INNER + grp * SSD_STATE
        c0 = SSD_INNER + SSD_GROUPS * SSD_STATE + grp * SSD_STATE
        bg = xc[:, b0:b0 + SSD_STATE].astype(BF16)
        cg = xc[:, c0:c0 + SSD_STATE].astype(BF16)
        cb = _dot_nt(cg, bg)
        for hh in range(heads_per_group):
            h = grp * heads_per_group + hh
            col = SMALL_DT + h
            ac = acum_col[:, col:col + 1]
            ar = acum_row[col:col + 1, :]
            decay = jnp.exp(jnp.where(causal, ac - ar, NEG))
            scores = (cb * decay).astype(BF16)
            xdt = xs[:, h * SSD_HEAD_DIM:(h + 1) * SSD_HEAD_DIM] * dt_col[:, col:col + 1]
            state = h_ref[h]
            y_h = _dot(scores, xdt.astype(BF16)) + _dot_nt(cg, state.astype(BF16)) * jnp.exp(ac)
            al = a_last[:, col:col + 1]
            xw = (xdt * jnp.exp(al - ac)).astype(BF16)
            h_ref[h] = jnp.exp(al) * state + lax.dot_general(xw, bg, TN_DIMS, preferred_element_type=F32)
            ys.append(y_h)
    y = jnp.concatenate(ys, axis=1) + dskip_ref[...] * xs
    y_ref[0] = _rms(y * _silu(z_ref[0]), g_ref[...])
    hfin_ref[0] = h_ref[...]


def _ssd(xbc, z, small, small_t, conv_init, h0, conv_w, conv_b, a_lane, a_col, dskip, g, L):
    b, n, _ = xbc.shape
    seq = lambda w: pl.BlockSpec((1, L, w), lambda i, j: (i, j, 0))
    per_b = lambda *s: pl.BlockSpec((1,) + s, lambda i, j: (i,) + (0,) * len(s))
    return pl.pallas_call(
        functools.partial(_ssd_kernel, L=L),
        grid=(b, n // L),
        in_specs=[seq(CONV_DIM), seq(SSD_INNER), seq(LANES),
                  pl.BlockSpec((1, small_t.shape[1], L), lambda i, j: (i, 0, j)),
                  per_b(CONV_PAD, CONV_DIM), per_b(SSD_HEADS, SSD_HEAD_DIM, SSD_STATE),
                  _const_spec(conv_w.shape), _const_spec(conv_b.shape), _const_spec(a_lane.shape),
                  _const_spec(a_col.shape), _const_spec(dskip.shape), _const_spec(g.shape)],
        out_specs=[seq(SSD_INNER), per_b(SSD_HEADS, SSD_HEAD_DIM, SSD_STATE)],
        out_shape=[jax.ShapeDtypeStruct((b, n, SSD_INNER), F32),
                   jax.ShapeDtypeStruct((b, SSD_HEADS, SSD_HEAD_DIM, SSD_STATE), F32)],
        scratch_shapes=[pltpu.VMEM((L + CONV_PAD, CONV_DIM), F32),
                        pltpu.VMEM((SSD_HEADS, SSD_HEAD_DIM, SSD_STATE), F32)],
        compiler_params=_params("parallel", "arbitrary"),
        name="ssd",
    )(xbc, z, small, small_t, conv_init, h0, conv_w, conv_b, a_lane, a_col, dskip, g)


def _post_mix_kernel(h_ref, sb_ref, fx_ref, ssd_ref, wout_ref, gx_ref, wq_ref, mk_ref, mv_ref, wo_ref, o_ref):
    d = h_ref.shape[2]
    xd = d // X_HEADS
    mix = (_dot(sb_ref[0].astype(BF16), wout_ref[0:ATT_WIDTH, :])
           + _dot(fx_ref[0].astype(BF16), wout_ref[ATT_WIDTH:2 * ATT_WIDTH, :])
           + _dot(ssd_ref[0].astype(BF16), wout_ref[2 * ATT_WIDTH:, :]))
    h1 = h_ref[0] + mix
    u = _rms(h1, gx_ref[...]).astype(BF16)
    q = (_dot(u, wq_ref[...]) * (xd ** -0.5)).astype(BF16)
    outs = []
    for hd in range(X_HEADS):
        cols = slice(hd * xd, (hd + 1) * xd)
        s = _dot_nt(q[:, cols], mk_ref[0, :, cols].astype(BF16))
        p = jnp.exp(s - jnp.max(s, axis=1, keepdims=True))
        o = _dot(p.astype(BF16), mv_ref[0, :, cols].astype(BF16))
        outs.append((o * (1.0 / jnp.sum(p, axis=1, keepdims=True))).astype(BF16))
    o_ref[0] = h1 + _dot(jnp.concatenate(outs, axis=1), wo_ref[...])


def _post_mix(h, sb_o, fox_o, ssd_o, w_out, gx, wq, mk, mv, wo, tm):
    b, n, d = h.shape
    seq = lambda w: pl.BlockSpec((1, tm, w), lambda i, j: (i, j, 0))
    mem_spec = pl.BlockSpec((1,) + mk.shape[1:], lambda i, j: (i, 0, 0))
    return pl.pallas_call(
        _post_mix_kernel,
        grid=(b, n // tm),
        in_specs=[seq(d), seq(ATT_WIDTH), seq(ATT_WIDTH), seq(SSD_INNER), _const_spec(w_out.shape),
                  _const_spec(gx.shape), _const_spec(wq.shape), mem_spec, mem_spec, _const_spec(wo.shape)],
        out_specs=seq(d),
        out_shape=jax.ShapeDtypeStruct(h.shape, F32),
        compiler_params=_params("parallel", "parallel"),
        name="post_mix",
    )(h, sb_o, fox_o, ssd_o, w_out, gx, wq, mk, mv, wo)


def _ffn_kernel(h_ref, g_ref, wg_ref, wu_ref, wd_ref, gf_ref, o_ref, *, final_norm):
    h = h_ref[...]
    u = _rms(h, g_ref[...]).astype(BF16)
    hidden = (_silu(_dot(u, wg_ref[...])) * _dot(u, wu_ref[...])).astype(BF16)
    out = h + _dot(hidden, wd_ref[...])
    if final_norm:
        out = _rms(out, gf_ref[...])
    o_ref[...] = out


def _ffn(h, g, wg, wu, wd, g_final, final_norm, tm):
    m, d = h.shape
    row = pl.BlockSpec((tm, d), lambda i: (i, 0))
    return pl.pallas_call(
        functools.partial(_ffn_kernel, final_norm=final_norm),
        grid=(m // tm,),
        in_specs=[row, _const_spec(g.shape), _const_spec(wg.shape), _const_spec(wu.shape),
                  _const_spec(wd.shape), _const_spec(g_final.shape)],
        out_specs=row,
        out_shape=jax.ShapeDtypeStruct(h.shape, F32),
        compiler_params=_params("parallel"),
        name="ffn",
    )(h, g, wg, wu, wd, g_final)


def _mem_kv_kernel(x_ref, g_ref, wk_ref, wv_ref, k_ref, v_ref):
    u = _rms(x_ref[...], g_ref[...]).astype(BF16)
    k_ref[...] = _dot(u, wk_ref[...])
    v_ref[...] = _dot(u, wv_ref[...])


def _mem_kv(mem, g, wk, wv, tm):
    m, d = mem.shape
    row = pl.BlockSpec((tm, d), lambda i: (i, 0))
    return pl.pallas_call(
        _mem_kv_kernel,
        grid=(m // tm,),
        in_specs=[row, _const_spec(g.shape), _const_spec(wk.shape), _const_spec(wv.shape)],
        out_specs=[row, row],
        out_shape=[jax.ShapeDtypeStruct((m, d), F32)] * 2,
        compiler_params=_params("parallel"),
        name="mem_kv",
    )(mem, g, wk, wv)


def _seq_minor(a):
    b, n = a.shape[:2]
    return jnp.transpose(a, (0, 2, 3, 1)).reshape(b, -1, n)


def _layer(h, p, past, mem_k, mem_v, final_g, final_norm):
    b, n, d = h.shape
    m = b * n
    prompt = past is None
    tm = min(m, 512)
    outs = _in_proj(h.reshape(m, d), p["norm_mix_g"], p, tm, b if prompt else None)
    sq, fq, z, xbc, small = (a.reshape(b, n, a.shape[-1]) for a in outs[:5])
    small_t = outs[5]
    kv_t = outs[6:]
    if not prompt:
        per_batch = lambda a: jnp.transpose(a.reshape(a.shape[0], b, n), (1, 0, 2))
        small_t = per_batch(small_t)
        kv_t = [per_batch(a) for a in kv_t]
    skt, svt, fkt, fvt, skb, svb, fkb, fvb = kv_t
    logf_t = small_t[:, SMALL_LOGF:SMALL_LOGF + ATT_HEADS, :]

    if prompt:
        fcum = _seq_cumsum(small_t[:, :CONV_PAD, :])
        sb_o = _sb_attn(sq, skb, svb, skb, svb, True)
        fox_o = _fox_attn(fq, fkb, fvb, fkb, fvb, fcum, True)
        conv_init = jnp.zeros((b, CONV_PAD, CONV_DIM), F32)
        h0 = jnp.zeros((b, SSD_HEADS, SSD_HEAD_DIM, SSD_STATE), F32)
        conv_hist = None
        ssd_len = ATT_BLOCK
    else:
        pad_keys = lambda a: jnp.pad(a, ((0, 0), (0, 0), (0, ATT_BLOCK - n)))
        sb_o = _sb_attn(sq, pad_keys(skb), pad_keys(svb), _seq_minor(past["sb_k"]), _seq_minor(past["sb_v"]),
                        False)
        logf_all = jnp.concatenate([jnp.transpose(past["fox_logf"], (0, 2, 1)), logf_t], axis=2)
        logf_all = jnp.pad(logf_all, ((0, 0), (0, CONV_PAD - ATT_HEADS), (0, ATT_BLOCK - n)))
        fcum = _seq_cumsum(logf_all)
        fox_o = _fox_attn(fq, pad_keys(fkb), pad_keys(fvb), _seq_minor(past["fox_k"]),
                          _seq_minor(past["fox_v"]), fcum, False)
        conv_hist = past["conv"]
        conv_init = jnp.pad(conv_hist, ((0, 0), (CONV_PAD - (SSD_CONV - 1), 0), (0, 0)))
        h0 = past["ssm"]
        ssd_len = n

    ssd_o, h_final = _ssd(xbc, z, small, small_t, conv_init, h0, p["conv_w"], p["conv_b"],
                          p["a_lane"], p["a_col"], p["dskip_vec"], p["ssd_norm_g"], ssd_len)
    if conv_hist is None:
        conv_state = xbc[:, n - (SSD_CONV - 1):, :]
    else:
        conv_state = jnp.concatenate([conv_hist, xbc], axis=1)[:, -(SSD_CONV - 1):, :]

    h1 = _post_mix(h, sb_o, fox_o, ssd_o, p["w_out"], p["norm_x_g"], p["wq_x"], mem_k, mem_v, p["wo_x"],
                   min(n, 512))
    h2 = _ffn(h1.reshape(m, d), p["norm_ffn_g"], p["w_gate"], p["w_up"], p["w_down"], final_g, final_norm, tm)
    return h2.reshape(b, n, d), (skt, svt, fkt, fvt, logf_t, h_final, conv_state)


def _layer_params(l, norm_mix_g, w_in_t, fox_b_f, conv_w, conv_b, dt_bias, a_log, d_skip, ssd_norm_g, w_out,
                  norm_x_g, wq_x, wo_x, norm_ffn_g, w_gate, w_up, w_down):
    aw = ATT_WIDTH
    qkv = 6 * aw
    gate_end = qkv + ATT_HEADS
    z_end = gate_end + SSD_INNER
    xbc_end = z_end + CONV_DIM
    w = w_in_t[l]
    d = w.shape[1]
    row = lambda a: a.reshape(1, -1).astype(F32)
    n_small = ATT_HEADS + SSD_HEADS
    small_pad = LANES - n_small
    a_neg = -jnp.exp(a_log[l].astype(F32))
    a_small = jnp.concatenate([jnp.zeros((ATT_HEADS,), F32), a_neg, jnp.zeros((small_pad,), F32)])
    b_small = jnp.concatenate([fox_b_f[l], dt_bias[l], jnp.zeros((small_pad,), F32)]).astype(F32)
    return dict(
        norm_mix_g=row(norm_mix_g[l]),
        wq_t=jnp.concatenate([w[0:aw], w[3 * aw:4 * aw]], axis=0).astype(BF16),
        wkv_t=jnp.concatenate([w[aw:3 * aw], w[4 * aw:6 * aw]], axis=0).astype(BF16),
        wzx_t=w[gate_end:xbc_end].astype(BF16),
        ws_t=jnp.concatenate([w[qkv:gate_end], w[xbc_end:], jnp.zeros((small_pad, d), w.dtype)],
                             axis=0).astype(BF16),
        b_small_row=b_small.reshape(1, LANES), b_small_col=b_small[:2 * CONV_PAD].reshape(2 * CONV_PAD, 1),
        conv_w=conv_w[l].astype(F32), conv_b=row(conv_b[l]),
        a_lane=a_small.reshape(1, LANES), a_col=a_small[:2 * CONV_PAD].reshape(2 * CONV_PAD, 1),
        dskip_vec=row(jnp.repeat(d_skip[l], SSD_HEAD_DIM)), ssd_norm_g=row(ssd_norm_g[l]),
        w_out=w_out[l].astype(BF16), norm_x_g=row(norm_x_g[l]),
        wq_x=wq_x[l].astype(BF16), wo_x=wo_x[l].astype(BF16), norm_ffn_g=row(norm_ffn_g[l]),
        w_gate=w_gate[l].astype(BF16), w_up=w_up[l].astype(BF16), w_down=w_down[l].astype(BF16))


def kernel(x_prompt, x_sample, cache_sb_k, cache_sb_v, cache_fox_k, cache_fox_v, cache_fox_logf, state_ssm, state_conv, cache_mem_k, cache_mem_v, mem_prompt, norm_mix_g, w_in, fox_b_f, conv_w, conv_b, dt_bias, a_log, d_skip, ssd_norm_g, w_out, norm_x_g, mem_norm_g, wq_x, wk_x, wv_x, wo_x, norm_ffn_g, w_gate, w_up, w_down, final_norm_g):
    depth = w_in.shape[0]
    bp, mem_len, d = mem_prompt.shape
    bs = x_sample.shape[0]
    final_g = final_norm_g.reshape(1, d).astype(F32)
    w_in_t = jnp.transpose(w_in, (0, 2, 1))
    hp, hs = x_prompt, x_sample
    p_st, s_st, p_mk, p_mv = [], [], [], []
    for l in range(depth):
        p = _layer_params(l, norm_mix_g, w_in_t, fox_b_f, conv_w, conv_b, dt_bias, a_log, d_skip, ssd_norm_g,
                          w_out, norm_x_g, wq_x, wo_x, norm_ffn_g, w_gate, w_up, w_down)
        last = l == depth - 1
        mk, mv = _mem_kv(mem_prompt.reshape(bp * mem_len, d), mem_norm_g[l].reshape(1, d).astype(F32),
                         wk_x[l].astype(BF16), wv_x[l].astype(BF16), 512)
        mk = mk.reshape(bp, mem_len, d)
        mv = mv.reshape(bp, mem_len, d)
        hp, st = _layer(hp, p, None, mk, mv, final_g, last)
        p_st.append(st)
        p_mk.append(mk.reshape(bp, mem_len, X_HEADS, d // X_HEADS))
        p_mv.append(mv.reshape(bp, mem_len, X_HEADS, d // X_HEADS))
        past = dict(sb_k=cache_sb_k[l], sb_v=cache_sb_v[l], fox_k=cache_fox_k[l], fox_v=cache_fox_v[l],
                    fox_logf=cache_fox_logf[l], ssm=state_ssm[l], conv=state_conv[l])
        hs, st = _layer(hs, p, past, cache_mem_k[l].reshape(bs, mem_len, d),
                        cache_mem_v[l].reshape(bs, mem_len, d), final_g, last)
        s_st.append(st)
    stk = lambda sts, i: jnp.stack([s[i] for s in sts], axis=0)
    kv = lambda sts, i: jnp.transpose(
        stk(sts, i).reshape(depth, sts[0][i].shape[0], ATT_HEADS, HEAD_DIM, -1), (0, 1, 4, 2, 3))
    logf = lambda sts: jnp.transpose(stk(sts, 4), (0, 1, 3, 2))
    return (hp, hs,
            kv(p_st, 0), kv(p_st, 1), kv(p_st, 2), kv(p_st, 3), logf(p_st), stk(p_st, 5), stk(p_st, 6),
            jnp.stack(p_mk, axis=0), jnp.stack(p_mv, axis=0),
            kv(s_st, 0), kv(s_st, 1), kv(s_st, 2), kv(s_st, 3), logf(s_st), stk(s_st, 5), stk(s_st, 6))
```

```python
import functools

import jax
import jax.numpy as jnp
from jax import lax
from jax.experimental import pallas as pl
from jax.experimental.pallas import tpu as pltpu

F32 = jnp.float32
BF16 = jnp.bfloat16
EPS = 1e-6

HEAD_DIM = 64
ATT_HEADS = 4
ATT_WIDTH = ATT_HEADS * HEAD_DIM
SSD_HEADS = 8
SSD_HEAD_DIM = 64
SSD_INNER = SSD_HEADS * SSD_HEAD_DIM
SSD_GROUPS = 2
SSD_STATE = 128
SSD_CONV = 4
CONV_DIM = SSD_INNER + 2 * SSD_GROUPS * SSD_STATE
X_HEADS = 4
LANES = 128
SMALL_LOGF = 0
SMALL_DT = ATT_HEADS
CONV_PAD = 8
ATT_BLOCK = 256
SAMPLE_PAST_BLOCK = 1024
NEG = -1e30
LOG2E = 1.4426950408889634
SB_DEAD_LOG2 = 150.0
VMEM_LIMIT = 56 * 1024 * 1024

NT_DIMS = (((1,), (1,)), ((), ()))
TN_DIMS = (((0,), (0,)), ((), ()))


def _params(*sem):
    return pltpu.CompilerParams(dimension_semantics=sem, vmem_limit_bytes=VMEM_LIMIT)


def _rms(x, g):
    ms = jnp.mean(x * x, axis=-1, keepdims=True)
    return x * lax.rsqrt(ms + EPS) * g


def _softplus(x):
    t = jnp.exp2(jnp.abs(x) * (-LOG2E))
    return jnp.maximum(x, 0.0) + jnp.log(1.0 + t)


def _softplus2(x):
    return jnp.maximum(x, 0.0) + jnp.log2(1.0 + jnp.exp2(-jnp.abs(x)))


def _silu(x):
    return x * (1.0 / (1.0 + jnp.exp(-x)))


def _dot(a, b):
    return jnp.dot(a, b, preferred_element_type=F32)


def _dot_nt(a, b):
    return lax.dot_general(a, b, NT_DIMS, preferred_element_type=F32)


def _split3(x):
    hi = x.astype(BF16)
    r1 = x - hi.astype(F32)
    mid = r1.astype(BF16)
    lo = (r1 - mid.astype(F32)).astype(BF16)
    return hi, mid, lo


def _const_spec(shape):
    return pl.BlockSpec(shape, lambda *_: (0,) * len(shape))


def _in_proj_kernel(x_ref, g_ref, wq_ref, wkv_ref, wzx_ref, ws_ref, bs_row_ref, bs_col_ref,
                    sq_ref, fq_ref, z_ref, xbc_ref, small_ref, small_t_ref,
                    skt_ref, svt_ref, fkt_ref, fvt_ref, skb_ref, svb_ref, fkb_ref, fvb_ref):
    u = _rms(x_ref[...], g_ref[...]).astype(BF16)
    q = _dot_nt(u, wq_ref[...]) * (LOG2E * HEAD_DIM ** -0.5)
    sq_ref[...] = q[:, :ATT_WIDTH].astype(BF16)
    fq_ref[...] = q[:, ATT_WIDTH:].astype(BF16)
    zx = _dot_nt(u, wzx_ref[...])
    z_ref[...] = zx[:, :SSD_INNER]
    xbc_ref[...] = zx[:, SSD_INNER:]
    kvt = _dot_nt(wkv_ref[...], u)
    for i, (f_ref, b_ref) in enumerate(((skt_ref, skb_ref), (svt_ref, svb_ref),
                                        (fkt_ref, fkb_ref), (fvt_ref, fvb_ref))):
        part = kvt[i * ATT_WIDTH:(i + 1) * ATT_WIDTH, :]
        f_ref[...] = part.reshape(f_ref.shape)
        b_ref[...] = part.astype(BF16).reshape(b_ref.shape)

    def gates(s, index):
        is_gate = index < SMALL_DT
        sp = _softplus(jnp.where(is_gate, -s, s))
        return jnp.where(is_gate, -sp, sp)

    s = _dot_nt(u, ws_ref[...]) + bs_row_ref[...]
    small_ref[...] = gates(s, lax.broadcasted_iota(jnp.int32, s.shape, 1))
    st = _dot_nt(ws_ref[0:2 * CONV_PAD, :], u) + bs_col_ref[...]
    small_t_ref[...] = gates(st, lax.broadcasted_iota(jnp.int32, st.shape, 0)).reshape(small_t_ref.shape)


def _in_proj(x, g, p, tm, batch):
    m, d = x.shape
    row = lambda n: pl.BlockSpec((tm, n), lambda i: (i, 0))
    if batch is None:
        t_shape = lambda r: (r, m)
        t_spec = lambda r: pl.BlockSpec((r, tm), lambda i: (0, i))
    else:
        nb = m // batch // tm
        t_shape = lambda r: (batch, r, m // batch)
        t_spec = lambda r: pl.BlockSpec((1, r, tm), lambda i: (i // nb, 0, i % nb))
    rows = [(ATT_WIDTH, BF16), (ATT_WIDTH, BF16), (SSD_INNER, F32), (CONV_DIM, F32), (LANES, F32)]
    cols = [(2 * CONV_PAD, F32)] + [(ATT_WIDTH, F32)] * 4 + [(ATT_WIDTH, BF16)] * 4
    weights = [p["wq_t"], p["wkv_t"], p["wzx_t"], p["ws_t"], p["b_small_row"], p["b_small_col"]]
    return pl.pallas_call(
        _in_proj_kernel,
        grid=(m // tm,),
        in_specs=[row(d), _const_spec((1, d))] + [_const_spec(w.shape) for w in weights],
        out_specs=[row(n) for n, _ in rows] + [t_spec(r) for r, _ in cols],
        out_shape=[jax.ShapeDtypeStruct((m, n), dt) for n, dt in rows]
        + [jax.ShapeDtypeStruct(t_shape(r), dt) for r, dt in cols],
        compiler_params=_params("parallel"),
        name="in_proj",
    )(x, g, *weights)


def _cumsum_kernel(x_ref, o_ref, *, blk):
    n = x_ref.shape[2]
    r = lax.broadcasted_iota(jnp.int32, (blk, blk), 0)
    c = lax.broadcasted_iota(jnp.int32, (blk, blk), 1)
    tri = jnp.where(r <= c, 1.0, 0.0).astype(BF16)
    tri3 = jnp.concatenate([tri, tri, tri], axis=0)
    carry = jnp.zeros((x_ref.shape[1], 1), F32)
    for j in range(n // blk):
        xb = x_ref[0, :, j * blk:(j + 1) * blk]
        out = _dot(jnp.concatenate(_split3(xb), axis=1), tri3) + carry
        o_ref[0, :, j * blk:(j + 1) * blk] = out
        carry = out[:, blk - 1:blk]


def _seq_cumsum(x):
    b, r, n = x.shape
    spec = pl.BlockSpec((1, r, n), lambda i: (i, 0, 0))
    return pl.pallas_call(
        functools.partial(_cumsum_kernel, blk=ATT_BLOCK),
        grid=(b,),
        in_specs=[spec],
        out_specs=spec,
        out_shape=jax.ShapeDtypeStruct(x.shape, F32),
        compiler_params=_params("parallel"),
        name="seq_cumsum",
    )(x)


def _stack_heads(q, qbd_ref, tq):
    lane_head = lax.broadcasted_iota(jnp.int32, q.shape, 1) // HEAD_DIM
    for h in range(ATT_HEADS):
        qbd_ref[h * tq:(h + 1) * tq, :] = jnp.where(lane_head == h, q, jnp.zeros_like(q))


def _unstack_heads(acc, tq):
    lane_head = lax.broadcasted_iota(jnp.int32, (tq, ATT_WIDTH), 1) // HEAD_DIM
    out = jnp.zeros((tq, ATT_WIDTH), F32)
    for h in range(ATT_HEADS):
        out = jnp.where(lane_head == h, acc[h * tq:(h + 1) * tq, :], out)
    return out


def _local_positions(rows, tq, tk):
    t_loc = lax.broadcasted_iota(jnp.int32, (rows, tk), 0) & (tq - 1)
    s_loc = lax.broadcasted_iota(jnp.int32, (rows, tk), 1)
    return t_loc, s_loc


def _lane_tile(x, width):
    return jnp.concatenate([x] * (width // LANES), axis=1)


def _past_loop(block, n_blocks, tp, refs, keep_going=None):
    def run(step):
        start = pl.multiple_of((n_blocks - 1 - step) * tp, tp)
        block(*(r[:, pl.ds(start, tp)] for r in refs))

    if keep_going is None:
        def body(step, carry):
            run(step)
            return carry

        lax.fori_loop(0, n_blocks, body, 0)
    else:
        def body(state):
            run(state[0])
            return state[0] + 1, keep_going()

        lax.while_loop(lambda state: (state[0] < n_blocks) & (state[1] > 0), body,
                       (jnp.int32(0), keep_going()))


def _attn_specs(tq, td, prompt, layer):
    q_spec = pl.BlockSpec((None, tq, ATT_WIDTH), lambda i, j: (i, j, 0))
    if prompt:
        edge = lambda r: pl.BlockSpec((None, r, td), lambda i, j: (i, 0, j))
        past = lambda a: pl.BlockSpec((None,) + a.shape[1:], lambda i, j: (i, 0, 0))
    else:
        edge = lambda r: pl.BlockSpec((None, r, td), lambda i, j: (i, 0, 0))
        past = lambda a: pl.BlockSpec((None, None) + a.shape[2:], lambda i, j: (layer, i, 0, 0))
    return q_spec, edge, past


def _sb_kernel(q_ref, kd_ref, vd_ref, kp_ref, vp_ref, o_ref, qbd_ref, acc_ref, carry_ref,
               *, tq, tp, n_past):
    rows = ATT_HEADS * tq
    _stack_heads(q_ref[...], qbd_ref, tq)
    acc_ref[...] = jnp.zeros_like(acc_ref)
    carry_ref[...] = jnp.zeros_like(carry_ref)
    r = lax.broadcasted_iota(jnp.int32, (2 * LANES, 2 * LANES), 0) & (LANES - 1)
    c = lax.broadcasted_iota(jnp.int32, (2 * LANES, 2 * LANES), 1)
    wcum = jnp.where((c >= LANES) | (r > c), -1.0, 0.0).astype(BF16)

    def block(kt, vt, diagonal=False):
        kt = kt.astype(BF16)
        vt = vt.astype(BF16)
        tk = kt.shape[1]
        s = _dot(qbd_ref[...], kt)
        sp = _softplus2(s)
        log_beta = s - sp
        if diagonal:
            t_loc, s_loc = _local_positions(rows, tq, tk)
            valid = s_loc < t_loc
            sp = jnp.where(valid, sp, 0.0)
        hi = sp.astype(BF16)
        lo = (sp - hi.astype(F32)).astype(BF16)
        carry = carry_ref[...]
        after = [None] * (tk // LANES)
        for g in reversed(range(tk // LANES)):
            cols = slice(g * LANES, (g + 1) * LANES)
            a = _dot(jnp.concatenate([hi[:, cols], lo[:, cols]], axis=1), wcum)
            after[g] = a[:, :LANES] + carry
            carry = carry + a[:, LANES:]
        carry_ref[...] = carry
        w = jnp.exp2(log_beta + jnp.concatenate(after, axis=1))
        if diagonal:
            w = jnp.where(valid, w, 0.0)
        acc_ref[...] += _dot_nt(w.astype(BF16), vt)

    def any_weight_left():
        return (jnp.max(carry_ref[...]) > -SB_DEAD_LOG2).astype(jnp.int32)

    block(kd_ref[...], vd_ref[...], diagonal=True)
    n_blocks = pl.program_id(1) if n_past is None else n_past
    _past_loop(block, n_blocks, tp, (kp_ref, vp_ref), keep_going=any_weight_left)
    o_ref[...] = _unstack_heads(acc_ref[...], tq)


def _sb_attn(q, kd, vd, kp, vp, layer=None):
    b, n, _ = q.shape
    prompt = layer is None
    td = ATT_BLOCK
    tq = ATT_BLOCK if prompt else n
    rows = ATT_HEADS * tq
    q_spec, edge, past = _attn_specs(tq, td, prompt, layer)
    n_past = None if prompt else kp.shape[-1] // td
    return pl.pallas_call(
        functools.partial(_sb_kernel, tq=tq, tp=td, n_past=n_past),
        grid=(b, n // tq),
        in_specs=[q_spec, edge(ATT_WIDTH), edge(ATT_WIDTH), past(kp), past(vp)],
        out_specs=q_spec,
        out_shape=jax.ShapeDtypeStruct((b, n, ATT_WIDTH), F32),
        scratch_shapes=[pltpu.VMEM((rows, ATT_WIDTH), BF16), pltpu.VMEM((rows, ATT_WIDTH), F32),
                        pltpu.VMEM((rows, LANES), F32)],
        compiler_params=_params("parallel", "arbitrary"),
        name="sb_attn",
    )(q, kd, vd, kp, vp)


def _fox_kernel(q_ref, kd_ref, vd_ref, fd_ref, kp_ref, vp_ref, fp_ref, o_ref,
                qbd_ref, acc_ref, m_ref, l_ref, *, tq, td, tp, n_past):
    rows = ATT_HEADS * tq
    _stack_heads(q_ref[...], qbd_ref, tq)
    acc_ref[...] = jnp.zeros_like(acc_ref)
    m_ref[...] = jnp.full_like(m_ref, NEG)
    l_ref[...] = jnp.zeros_like(l_ref)

    def block(kt, vt, fk, diagonal=False):
        kt = kt.astype(BF16)
        vt = vt.astype(BF16)
        fk = fk * LOG2E
        tk = kt.shape[1]
        s = _dot(qbd_ref[...], kt)
        sh = jnp.concatenate([s[h * tq:(h + 1) * tq] - fk[h:h + 1, :] for h in range(ATT_HEADS)], axis=0)
        if diagonal:
            t_loc, s_loc = _local_positions(rows, tq, tk)
            sh = jnp.where(s_loc <= t_loc, sh, NEG)
        m_old = m_ref[...]
        m_new = jnp.maximum(m_old, jnp.max(sh, axis=1, keepdims=True))
        alpha = jnp.exp2(m_old - m_new)
        p = jnp.exp2(sh - _lane_tile(m_new, tk))
        l_ref[...] = alpha * l_ref[...] + jnp.sum(p, axis=1, keepdims=True)
        m_ref[...] = m_new
        acc_ref[...] = acc_ref[...] * _lane_tile(alpha, ATT_WIDTH) + _dot_nt(p.astype(BF16), vt)

    block(kd_ref[...], vd_ref[...], fd_ref[...], diagonal=True)
    past_refs = (kp_ref, vp_ref, fp_ref)
    if n_past is None:
        n_edge = pl.program_id(1)

        @pl.when(n_edge % 2 == 1)
        def _():
            start = pl.multiple_of((n_edge - 1) * td, td)
            block(*(r[:, pl.ds(start, td)] for r in past_refs))

        _past_loop(block, n_edge // 2, tp, past_refs)
    else:
        _past_loop(block, n_past, tp, past_refs)
    o_ref[...] = _unstack_heads(acc_ref[...] * _lane_tile(1.0 / l_ref[...], ATT_WIDTH), tq)


def _fox_attn(q, kd, vd, kp, vp, fcum, layer=None):
    b, n, _ = q.shape
    prompt = layer is None
    td = ATT_BLOCK
    tq = ATT_BLOCK if prompt else n
    tp = 2 * ATT_BLOCK if prompt else SAMPLE_PAST_BLOCK
    rows = ATT_HEADS * tq
    q_spec, edge, past = _attn_specs(tq, td, prompt, layer)
    n_past = None if prompt else kp.shape[-1] // tp
    fr = fcum.shape[1]
    if prompt:
        fd_spec = edge(fr)
    else:
        last = kp.shape[-1] // td
        fd_spec = pl.BlockSpec((None, fr, td), lambda i, j: (i, 0, last))
    fp_spec = pl.BlockSpec((None,) + fcum.shape[1:], lambda i, j: (i, 0, 0))
    return pl.pallas_call(
        functools.partial(_fox_kernel, tq=tq, td=td, tp=tp, n_past=n_past),
        grid=(b, n // tq),
        in_specs=[q_spec, edge(ATT_WIDTH), edge(ATT_WIDTH), fd_spec, past(kp), past(vp), fp_spec],
        out_specs=q_spec,
        out_shape=jax.ShapeDtypeStruct((b, n, ATT_WIDTH), F32),
        scratch_shapes=[pltpu.VMEM((rows, ATT_WIDTH), BF16), pltpu.VMEM((rows, ATT_WIDTH), F32),
                        pltpu.VMEM((rows, LANES), F32), pltpu.VMEM((rows, LANES), F32)],
        compiler_params=_params("parallel", "arbitrary"),
        name="fox_attn",
    )(q, kd, vd, fcum, kp, vp, fcum)


def _ssd_kernel(xbc_ref, z_ref, sm_ref, smt_ref, cinit_ref, h0_ref, cw_ref, cb_ref, alane_ref, acol_ref,
                dskip_ref, g_ref, y_ref, hfin_ref, xp_ref, h_ref, *, L):
    @pl.when(pl.program_id(1) == 0)
    def _():
        h_ref[...] = h0_ref[0]
        xp_ref[0:CONV_PAD, :] = cinit_ref[0]

    xp_ref[CONV_PAD:CONV_PAD + L, :] = xbc_ref[0]
    first = CONV_PAD - (SSD_CONV - 1)
    conv = cb_ref[...]
    for i in range(SSD_CONV):
        conv = conv + cw_ref[i:i + 1, :] * xp_ref[first + i:first + i + L, :]
    xp_ref[0:CONV_PAD, :] = xp_ref[L:L + CONV_PAD, :]
    xc = _silu(conv)
    xs = xc[:, :SSD_INNER]

    dt_col = sm_ref[0]
    a_col = dt_col * alane_ref[...]
    ri = lax.broadcasted_iota(jnp.int32, (L, L), 0)
    ci = lax.broadcasted_iota(jnp.int32, (L, L), 1)
    causal = ci <= ri
    tri_lower = jnp.where(causal, 1.0, 0.0).astype(BF16)
    tri_upper = jnp.where(ri <= ci, 1.0, 0.0).astype(BF16)
    acum_col = sum(_dot(tri_lower, part) for part in _split3(a_col))
    a_row = smt_ref[0] * acol_ref[...]
    acum_row = _dot(jnp.concatenate(_split3(a_row), axis=1),
                    jnp.concatenate([tri_upper] * 3, axis=0))
    a_last = acum_col[L - 1:L, :]

    ys = []
    heads_per_group = SSD_HEADS // SSD_GROUPS
    for grp in range(SSD_GROUPS):
        b0 = SSD_INNER + grp * SSD_STATE
        c0 = SSD_INNER + SSD_GROUPS * SSD_STATE + grp * SSD_STATE
        bg = xc[:, b0:b0 + SSD_STATE].astype(BF16)
        cg = xc[:, c0:c0 + SSD_STATE].astype(BF16)
        cb = _dot_nt(cg, bg)
        for hh in range(heads_per_group):
            h = grp * heads_per_group + hh
            col = SMALL_DT + h
            ac = acum_col[:, col:col + 1]
            ar = acum_row[col:col + 1, :]
            decay = jnp.exp(jnp.where(causal, ac - ar, NEG))
            scores = (cb * decay).astype(BF16)
            xdt = xs[:, h * SSD_HEAD_DIM:(h + 1) * SSD_HEAD_DIM] * dt_col[:, col:col + 1]
            state = h_ref[h]
            y_h = _dot(scores, xdt.astype(BF16)) + _dot_nt(cg, state.astype(BF16)) * jnp.exp(ac)
            al = a_last[:, col:col + 1]
            xw = (xdt * jnp.exp(al - ac)).astype(BF16)
            h_ref[h] = jnp.exp(al) * state + lax.dot_general(xw, bg, TN_DIMS, preferred_element_type=F32)
            ys.append(y_h)
    y = jnp.concatenate(ys, axis=1) + dskip_ref[...] * xs
    y_ref[0] = _rms(y * _silu(z_ref[0]), g_ref[...])
    hfin_ref[0] = h_ref[...]


def _ssd(xbc, z, small, small_t, conv_init, h0, conv_w, conv_b, a_lane, a_col, dskip, g, L):
    b, n, _ = xbc.shape
    seq = lambda w: pl.BlockSpec((1, L, w), lambda i, j: (i, j, 0))
    per_b = lambda *s: pl.BlockSpec((1,) + s, lambda i, j: (i,) + (0,) * len(s))
    return pl.pallas_call(
        functools.partial(_ssd_kernel, L=L),
        grid=(b, n // L),
        in_specs=[seq(CONV_DIM), seq(SSD_INNER), seq(LANES),
                  pl.BlockSpec((1, small_t.shape[1], L), lambda i, j: (i, 0, j)),
                  per_b(CONV_PAD, CONV_DIM), per_b(SSD_HEADS, SSD_HEAD_DIM, SSD_STATE),
                  _const_spec(conv_w.shape), _const_spec(conv_b.shape), _const_spec(a_lane.shape),
                  _const_spec(a_col.shape), _const_spec(dskip.shape), _const_spec(g.shape)],
        out_specs=[seq(SSD_INNER), per_b(SSD_HEADS, SSD_HEAD_DIM, SSD_STATE)],
        out_shape=[jax.ShapeDtypeStruct((b, n, SSD_INNER), F32),
                   jax.ShapeDtypeStruct((b, SSD_HEADS, SSD_HEAD_DIM, SSD_STATE), F32)],
        scratch_shapes=[pltpu.VMEM((L + CONV_PAD, CONV_DIM), F32),
                        pltpu.VMEM((SSD_HEADS, SSD_HEAD_DIM, SSD_STATE), F32)],
        compiler_params=_params("parallel", "arbitrary"),
        name="ssd",
    )(xbc, z, small, small_t, conv_init, h0, conv_w, conv_b, a_lane, a_col, dskip, g)


def _post_mix_kernel(h_ref, sb_ref, fx_ref, ssd_ref, wout_ref, gx_ref, wq_ref, mk_ref, mv_ref, wo_ref, o_ref):
    d = h_ref.shape[2]
    xd = d // X_HEADS
    mix = (_dot(sb_ref[0].astype(BF16), wout_ref[0:ATT_WIDTH, :])
           + _dot(fx_ref[0].astype(BF16), wout_ref[ATT_WIDTH:2 * ATT_WIDTH, :])
           + _dot(ssd_ref[0].astype(BF16), wout_ref[2 * ATT_WIDTH:, :]))
    h1 = h_ref[0] + mix
    u = _rms(h1, gx_ref[...]).astype(BF16)
    q = (_dot(u, wq_ref[...]) * (xd ** -0.5)).astype(BF16)
    outs = []
    for hd in range(X_HEADS):
        cols = slice(hd * xd, (hd + 1) * xd)
        s = _dot_nt(q[:, cols], mk_ref[0, :, cols].astype(BF16))
        p = jnp.exp(s - jnp.max(s, axis=1, keepdims=True))
        o = _dot(p.astype(BF16), mv_ref[0, :, cols].astype(BF16))
        outs.append((o * (1.0 / jnp.sum(p, axis=1, keepdims=True))).astype(BF16))
    o_ref[0] = h1 + _dot(jnp.concatenate(outs, axis=1), wo_ref[...])


def _post_mix(h, sb_o, fox_o, ssd_o, w_out, gx, wq, mk, mv, wo, tm):
    b, n, d = h.shape
    seq = lambda w: pl.BlockSpec((1, tm, w), lambda i, j: (i, j, 0))
    mem_spec = pl.BlockSpec((1,) + mk.shape[1:], lambda i, j: (i, 0, 0))
    return pl.pallas_call(
        _post_mix_kernel,
        grid=(b, n // tm),
        in_specs=[seq(d), seq(ATT_WIDTH), seq(ATT_WIDTH), seq(SSD_INNER), _const_spec(w_out.shape),
                  _const_spec(gx.shape), _const_spec(wq.shape), mem_spec, mem_spec, _const_spec(wo.shape)],
        out_specs=seq(d),
        out_shape=jax.ShapeDtypeStruct(h.shape, F32),
        compiler_params=_params("parallel", "parallel"),
        name="post_mix",
    )(h, sb_o, fox_o, ssd_o, w_out, gx, wq, mk, mv, wo)


def _ffn_kernel(h_ref, g_ref, wg_ref, wu_ref, wd_ref, gf_ref, o_ref, *, final_norm):
    h = h_ref[...]
    u = _rms(h, g_ref[...]).astype(BF16)
    hidden = (_silu(_dot(u, wg_ref[...])) * _dot(u, wu_ref[...])).astype(BF16)
    out = h + _dot(hidden, wd_ref[...])
    if final_norm:
        out = _rms(out, gf_ref[...])
    o_ref[...] = out


def _ffn(h, g, wg, wu, wd, g_final, final_norm, tm):
    m, d = h.shape
    row = pl.BlockSpec((tm, d), lambda i: (i, 0))
    return pl.pallas_call(
        functools.partial(_ffn_kernel, final_norm=final_norm),
        grid=(m // tm,),
        in_specs=[row, _const_spec(g.shape), _const_spec(wg.shape), _const_spec(wu.shape),
                  _const_spec(wd.shape), _const_spec(g_final.shape)],
        out_specs=row,
        out_shape=jax.ShapeDtypeStruct(h.shape, F32),
        compiler_params=_params("parallel"),
        name="ffn",
    )(h, g, wg, wu, wd, g_final)


def _mem_kv_kernel(x_ref, g_ref, wk_ref, wv_ref, k_ref, v_ref):
    u = _rms(x_ref[...], g_ref[...]).astype(BF16)
    k_ref[...] = _dot(u, wk_ref[...])
    v_ref[...] = _dot(u, wv_ref[...])


def _mem_kv(mem, g, wk, wv, tm):
    m, d = mem.shape
    row = pl.BlockSpec((tm, d), lambda i: (i, 0))
    return pl.pallas_call(
        _mem_kv_kernel,
        grid=(m // tm,),
        in_specs=[row, _const_spec(g.shape), _const_spec(wk.shape), _const_spec(wv.shape)],
        out_specs=[row, row],
        out_shape=[jax.ShapeDtypeStruct((m, d), F32)] * 2,
        compiler_params=_params("parallel"),
        name="mem_kv",
    )(mem, g, wk, wv)


def _seq_minor(a):
    layers, b, n = a.shape[:3]
    return jnp.transpose(a, (0, 1, 3, 4, 2)).reshape(layers, b, -1, n)


def _layer(h, p, past, mem_k, mem_v, final_g, final_norm):
    b, n, d = h.shape
    m = b * n
    prompt = past is None
    tm = min(m, 512)
    outs = _in_proj(h.reshape(m, d), p["norm_mix_g"], p, tm, b if prompt else None)
    sq, fq, z, xbc, small = (a.reshape(b, n, a.shape[-1]) for a in outs[:5])
    small_t = outs[5]
    kv_t = outs[6:]
    if not prompt:
        per_batch = lambda a: jnp.transpose(a.reshape(a.shape[0], b, n), (1, 0, 2))
        small_t = per_batch(small_t)
        kv_t = [per_batch(a) for a in kv_t]
    skt, svt, fkt, fvt, skb, svb, fkb, fvb = kv_t
    logf_t = small_t[:, SMALL_LOGF:SMALL_LOGF + ATT_HEADS, :]

    if prompt:
        fcum = _seq_cumsum(small_t[:, :CONV_PAD, :])
        sb_o = _sb_attn(sq, skb, svb, skb, svb)
        fox_o = _fox_attn(fq, fkb, fvb, fkb, fvb, fcum)
        conv_init = jnp.zeros((b, CONV_PAD, CONV_DIM), F32)
        h0 = jnp.zeros((b, SSD_HEADS, SSD_HEAD_DIM, SSD_STATE), F32)
        conv_hist = None
        ssd_len = ATT_BLOCK
    else:
        pad_keys = lambda a: jnp.pad(a, ((0, 0), (0, 0), (0, ATT_BLOCK - n)))
        layer = past["layer"]
        sb_o = _sb_attn(sq, pad_keys(skb), pad_keys(svb), past["sb_k"], past["sb_v"], layer)
        logf_all = jnp.concatenate([jnp.transpose(past["fox_logf"], (0, 2, 1)), logf_t], axis=2)
        logf_all = jnp.pad(logf_all, ((0, 0), (0, CONV_PAD - ATT_HEADS), (0, ATT_BLOCK - n)))
        fcum = _seq_cumsum(logf_all)
        fox_o = _fox_attn(fq, pad_keys(fkb), pad_keys(fvb), past["fox_k"], past["fox_v"], fcum, layer)
        conv_hist = past["conv"]
        conv_init = jnp.pad(conv_hist, ((0, 0), (CONV_PAD - (SSD_CONV - 1), 0), (0, 0)))
        h0 = past["ssm"]
        ssd_len = n

    ssd_o, h_final = _ssd(xbc, z, small, small_t, conv_init, h0, p["conv_w"], p["conv_b"],
                          p["a_lane"], p["a_col"], p["dskip_vec"], p["ssd_norm_g"], ssd_len)
    if conv_hist is None:
        conv_state = xbc[:, n - (SSD_CONV - 1):, :]
    else:
        conv_state = jnp.concatenate([conv_hist, xbc], axis=1)[:, -(SSD_CONV - 1):, :]

    h1 = _post_mix(h, sb_o, fox_o, ssd_o, p["w_out"], p["norm_x_g"], p["wq_x"], mem_k, mem_v, p["wo_x"],
                   min(n, 512))
    h2 = _ffn(h1.reshape(m, d), p["norm_ffn_g"], p["w_gate"], p["w_up"], p["w_down"], final_g, final_norm, tm)
    return h2.reshape(b, n, d), (skt, svt, fkt, fvt, logf_t, h_final, conv_state)


def _layer_params(l, norm_mix_g, w_in_t, fox_b_f, conv_w, conv_b, dt_bias, a_log, d_skip, ssd_norm_g, w_out,
                  norm_x_g, wq_x, wo_x, norm_ffn_g, w_gate, w_up, w_down):
    aw = ATT_WIDTH
    qkv = 6 * aw
    gate_end = qkv + ATT_HEADS
    z_end = gate_end + SSD_INNER
    xbc_end = z_end + CONV_DIM
    w = w_in_t[l]
    d = w.shape[1]
    row = lambda a: a.reshape(1, -1).astype(F32)
    n_small = ATT_HEADS + SSD_HEADS
    small_pad = LANES - n_small
    a_neg = -jnp.exp(a_log[l].astype(F32))
    a_small = jnp.concatenate([jnp.zeros((ATT_HEADS,), F32), a_neg, jnp.zeros((small_pad,), F32)])
    b_small = jnp.concatenate([fox_b_f[l], dt_bias[l], jnp.zeros((small_pad,), F32)]).astype(F32)
    return dict(
        norm_mix_g=row(norm_mix_g[l]),
        wq_t=jnp.concatenate([w[0:aw], w[3 * aw:4 * aw]], axis=0).astype(BF16),
        wkv_t=jnp.concatenate([w[aw:3 * aw], w[4 * aw:6 * aw]], axis=0).astype(BF16),
        wzx_t=w[gate_end:xbc_end].astype(BF16),
        ws_t=jnp.concatenate([w[qkv:gate_end], w[xbc_end:], jnp.zeros((small_pad, d), w.dtype)],
                             axis=0).astype(BF16),
        b_small_row=b_small.reshape(1, LANES), b_small_col=b_small[:2 * CONV_PAD].reshape(2 * CONV_PAD, 1),
        conv_w=conv_w[l].astype(F32), conv_b=row(conv_b[l]),
        a_lane=a_small.reshape(1, LANES), a_col=a_small[:2 * CONV_PAD].reshape(2 * CONV_PAD, 1),
        dskip_vec=row(jnp.repeat(d_skip[l], SSD_HEAD_DIM)), ssd_norm_g=row(ssd_norm_g[l]),
        w_out=w_out[l].astype(BF16), norm_x_g=row(norm_x_g[l]),
        wq_x=wq_x[l].astype(BF16), wo_x=wo_x[l].astype(BF16), norm_ffn_g=row(norm_ffn_g[l]),
        w_gate=w_gate[l].astype(BF16), w_up=w_up[l].astype(BF16), w_down=w_down[l].astype(BF16))


def kernel(x_prompt, x_sample, cache_sb_k, cache_sb_v, cache_fox_k, cache_fox_v, cache_fox_logf, state_ssm, state_conv, cache_mem_k, cache_mem_v, mem_prompt, norm_mix_g, w_in, fox_b_f, conv_w, conv_b, dt_bias, a_log, d_skip, ssd_norm_g, w_out, norm_x_g, mem_norm_g, wq_x, wk_x, wv_x, wo_x, norm_ffn_g, w_gate, w_up, w_down, final_norm_g):
    depth = w_in.shape[0]
    bp, mem_len, d = mem_prompt.shape
    bs = x_sample.shape[0]
    final_g = final_norm_g.reshape(1, d).astype(F32)
    w_in_t = jnp.transpose(w_in, (0, 2, 1))
    caches_t = [_seq_minor(c) for c in (cache_sb_k, cache_sb_v, cache_fox_k, cache_fox_v)]
    hp, hs = x_prompt, x_sample
    p_st, s_st, p_mk, p_mv = [], [], [], []
    for l in range(depth):
        p = _layer_params(l, norm_mix_g, w_in_t, fox_b_f, conv_w, conv_b, dt_bias, a_log, d_skip, ssd_norm_g,
                          w_out, norm_x_g, wq_x, wo_x, norm_ffn_g, w_gate, w_up, w_down)
        last = l == depth - 1
        mk, mv = _mem_kv(mem_prompt.reshape(bp * mem_len, d), mem_norm_g[l].reshape(1, d).astype(F32),
                         wk_x[l].astype(BF16), wv_x[l].astype(BF16), 512)
        mk = mk.reshape(bp, mem_len, d)
        mv = mv.reshape(bp, mem_len, d)
        hp, st = _layer(hp, p, None, mk, mv, final_g, last)
        p_st.append(st)
        p_mk.append(mk.reshape(bp, mem_len, X_HEADS, d // X_HEADS))
        p_mv.append(mv.reshape(bp, mem_len, X_HEADS, d // X_HEADS))
        past = dict(layer=l, sb_k=caches_t[0], sb_v=caches_t[1], fox_k=caches_t[2], fox_v=caches_t[3],
                    fox_logf=cache_fox_logf[l], ssm=state_ssm[l], conv=state_conv[l])
        hs, st = _layer(hs, p, past, cache_mem_k[l].reshape(bs, mem_len, d),
                        cache_mem_v[l].reshape(bs, mem_len, d), final_g, last)
        s_st.append(st)
    stk = lambda sts, i: jnp.stack([s[i] for s in sts], axis=0)
    kv = lambda sts, i: jnp.transpose(
        stk(sts, i).reshape(depth, sts[0][i].shape[0], ATT_HEADS, HEAD_DIM, -1), (0, 1, 4, 2, 3))
    logf = lambda sts: jnp.transpose(stk(sts, 4), (0, 1, 3, 2))
    return (hp, hs,
            kv(p_st, 0), kv(p_st, 1), kv(p_st, 2), kv(p_st, 3), logf(p_st), stk(p_st, 5), stk(p_st, 6),
            jnp.stack(p_mk, axis=0), jnp.stack(p_mv, axis=0),
            kv(s_st, 0), kv(s_st, 1), kv(s_st, 2), kv(s_st, 3), logf(s_st), stk(s_st, 5), stk(s_st, 6))
```

```python
import functools

import jax
import jax.numpy as jnp
from jax import lax
from jax.experimental import pallas as pl
from jax.experimental.pallas import tpu as pltpu

F32 = jnp.float32
BF16 = jnp.bfloat16
EPS = 1e-6

HEAD_DIM = 64
ATT_HEADS = 4
ATT_WIDTH = ATT_HEADS * HEAD_DIM
SSD_HEADS = 8
SSD_HEAD_DIM = 64
SSD_INNER = SSD_HEADS * SSD_HEAD_DIM
SSD_GROUPS = 2
SSD_STATE = 128
SSD_CONV = 4
CONV_DIM = SSD_INNER + 2 * SSD_GROUPS * SSD_STATE
X_HEADS = 4
LANES = 128
SMALL_LOGF = 0
SMALL_DT = ATT_HEADS
CONV_PAD = 8
ATT_BLOCK = 256
SAMPLE_PAST_BLOCK = 1024
NEG = -1e30
LOG2E = 1.4426950408889634
SB_DEAD_LOG2 = 150.0
VMEM_LIMIT = 56 * 1024 * 1024

NT_DIMS = (((1,), (1,)), ((), ()))
TN_DIMS = (((0,), (0,)), ((), ()))


def _params(*sem):
    return pltpu.CompilerParams(dimension_semantics=sem, vmem_limit_bytes=VMEM_LIMIT)


def _rms(x, g):
    ms = jnp.mean(x * x, axis=-1, keepdims=True)
    return x * lax.rsqrt(ms + EPS) * g


def _softplus(x):
    t = jnp.exp2(jnp.abs(x) * (-LOG2E))
    return jnp.maximum(x, 0.0) + jnp.log(1.0 + t)


def _softplus2(x):
    return jnp.maximum(x, 0.0) + jnp.log2(1.0 + jnp.exp2(-jnp.abs(x)))


def _silu(x):
    half = 0.5 * x
    return half + half * jnp.tanh(half)


def _dot(a, b):
    return jnp.dot(a, b, preferred_element_type=F32)


def _dot_nt(a, b):
    return lax.dot_general(a, b, NT_DIMS, preferred_element_type=F32)


def _split3(x):
    hi = x.astype(BF16)
    r1 = x - hi.astype(F32)
    mid = r1.astype(BF16)
    lo = (r1 - mid.astype(F32)).astype(BF16)
    return hi, mid, lo


def _wspec(a, layer=None):
    if layer is None:
        return pl.BlockSpec(a.shape, lambda *_: (0,) * a.ndim)
    return pl.BlockSpec((None,) + a.shape[1:], lambda *_: (layer,) + (0,) * (a.ndim - 1))


def _in_proj_kernel(x_ref, g_ref, wq_ref, wkv_ref, wzx_ref, ws_ref, bs_row_ref, bs_col_ref,
                    sq_ref, fq_ref, z_ref, xbc_ref, small_ref, small_t_ref,
                    skt_ref, svt_ref, fkt_ref, fvt_ref, skb_ref, svb_ref, fkb_ref, fvb_ref):
    u = _rms(x_ref[...], g_ref[...]).astype(BF16)
    q = _dot_nt(u, wq_ref[...]) * (LOG2E * HEAD_DIM ** -0.5)
    sq_ref[...] = q[:, :ATT_WIDTH].astype(BF16)
    fq_ref[...] = q[:, ATT_WIDTH:].astype(BF16)
    zx = _dot_nt(u, wzx_ref[...])
    z_ref[...] = zx[:, :SSD_INNER]
    xbc_ref[...] = zx[:, SSD_INNER:]
    kvt = _dot_nt(wkv_ref[...], u)
    for i, (f_ref, b_ref) in enumerate(((skt_ref, skb_ref), (svt_ref, svb_ref),
                                        (fkt_ref, fkb_ref), (fvt_ref, fvb_ref))):
        part = kvt[i * ATT_WIDTH:(i + 1) * ATT_WIDTH, :]
        f_ref[...] = part.reshape(f_ref.shape)
        b_ref[...] = part.astype(BF16).reshape(b_ref.shape)

    def gates(s, index):
        is_gate = index < SMALL_DT
        sp = _softplus(jnp.where(is_gate, -s, s))
        return jnp.where(is_gate, -sp, sp)

    s = _dot_nt(u, ws_ref[...]) + bs_row_ref[...]
    small_ref[...] = gates(s, lax.broadcasted_iota(jnp.int32, s.shape, 1))
    st = _dot_nt(ws_ref[0:2 * CONV_PAD, :], u) + bs_col_ref[...]
    small_t_ref[...] = gates(st, lax.broadcasted_iota(jnp.int32, st.shape, 0)).reshape(small_t_ref.shape)


def _in_proj(x, p, layer, tm, batch):
    m, d = x.shape
    row = lambda n: pl.BlockSpec((tm, n), lambda i: (i, 0))
    if batch is None:
        t_shape = lambda r: (r, m)
        t_spec = lambda r: pl.BlockSpec((r, tm), lambda i: (0, i))
    else:
        nb = m // batch // tm
        t_shape = lambda r: (batch, r, m // batch)
        t_spec = lambda r: pl.BlockSpec((1, r, tm), lambda i: (i // nb, 0, i % nb))
    rows = [(ATT_WIDTH, BF16), (ATT_WIDTH, BF16), (SSD_INNER, F32), (CONV_DIM, F32), (LANES, F32)]
    cols = [(2 * CONV_PAD, F32)] + [(ATT_WIDTH, F32)] * 4 + [(ATT_WIDTH, BF16)] * 4
    weights = [p["norm_mix_g"], p["wq_t"], p["wkv_t"], p["wzx_t"], p["ws_t"], p["b_small_row"], p["b_small_col"]]
    return pl.pallas_call(
        _in_proj_kernel,
        grid=(m // tm,),
        in_specs=[row(d)] + [_wspec(w, layer) for w in weights],
        out_specs=[row(n) for n, _ in rows] + [t_spec(r) for r, _ in cols],
        out_shape=[jax.ShapeDtypeStruct((m, n), dt) for n, dt in rows]
        + [jax.ShapeDtypeStruct(t_shape(r), dt) for r, dt in cols],
        compiler_params=_params("parallel"),
        name="in_proj",
    )(x, *weights)


def _cumsum_kernel(x_ref, o_ref, *, blk):
    n = x_ref.shape[2]
    r = lax.broadcasted_iota(jnp.int32, (blk, blk), 0)
    c = lax.broadcasted_iota(jnp.int32, (blk, blk), 1)
    tri = jnp.where(r <= c, 1.0, 0.0).astype(BF16)
    tri3 = jnp.concatenate([tri, tri, tri], axis=0)
    carry = jnp.zeros((x_ref.shape[1], 1), F32)
    for j in range(n // blk):
        xb = x_ref[0, :, j * blk:(j + 1) * blk]
        out = _dot(jnp.concatenate(_split3(xb), axis=1), tri3) + carry
        o_ref[0, :, j * blk:(j + 1) * blk] = out
        carry = out[:, blk - 1:blk]


def _seq_cumsum(x):
    b, r, n = x.shape
    spec = pl.BlockSpec((1, b * r, n), lambda i: (0, 0, 0))
    return pl.pallas_call(
        functools.partial(_cumsum_kernel, blk=ATT_BLOCK),
        grid=(1,),
        in_specs=[spec],
        out_specs=spec,
        out_shape=jax.ShapeDtypeStruct((1, b * r, n), F32),
        compiler_params=_params("arbitrary"),
        name="seq_cumsum",
    )(x.reshape(1, b * r, n)).reshape(b, r, n)


def _stack_heads(q, qbd_ref, tq):
    lane_head = lax.broadcasted_iota(jnp.int32, q.shape, 1) // HEAD_DIM
    for h in range(ATT_HEADS):
        qbd_ref[h * tq:(h + 1) * tq, :] = jnp.where(lane_head == h, q, jnp.zeros_like(q))


def _unstack_heads(acc, tq):
    lane_head = lax.broadcasted_iota(jnp.int32, (tq, ATT_WIDTH), 1) // HEAD_DIM
    out = jnp.zeros((tq, ATT_WIDTH), F32)
    for h in range(ATT_HEADS):
        out = jnp.where(lane_head == h, acc[h * tq:(h + 1) * tq, :], out)
    return out


def _local_positions(rows, tq, tk):
    t_loc = lax.broadcasted_iota(jnp.int32, (rows, tk), 0) & (tq - 1)
    s_loc = lax.broadcasted_iota(jnp.int32, (rows, tk), 1)
    return t_loc, s_loc


def _lane_tile(x, width):
    return jnp.concatenate([x] * (width // LANES), axis=1)


def _past_loop(block, n_blocks, tp, refs, keep_going=None):
    def run(step):
        start = pl.multiple_of((n_blocks - 1 - step) * tp, tp)
        block(*(r[:, pl.ds(start, tp)] for r in refs))

    if keep_going is None:
        def body(step, carry):
            run(step)
            return carry

        lax.fori_loop(0, n_blocks, body, 0)
    else:
        def body(state):
            run(state[0])
            return state[0] + 1, keep_going()

        lax.while_loop(lambda state: (state[0] < n_blocks) & (state[1] > 0), body,
                       (jnp.int32(0), keep_going()))


def _attn_specs(tq, td, prompt, layer):
    q_spec = pl.BlockSpec((None, tq, ATT_WIDTH), lambda i, j: (i, j, 0))
    if prompt:
        edge = lambda r: pl.BlockSpec((None, r, td), lambda i, j: (i, 0, j))
        past = lambda a: pl.BlockSpec((None,) + a.shape[1:], lambda i, j: (i, 0, 0))
    else:
        edge = lambda r: pl.BlockSpec((None, r, td), lambda i, j: (i, 0, 0))
        past = lambda a: pl.BlockSpec((None, None) + a.shape[2:], lambda i, j: (layer, i, 0, 0))
    return q_spec, edge, past


def _sb_kernel(q_ref, kd_ref, vd_ref, kp_ref, vp_ref, o_ref, qbd_ref, acc_ref, carry_ref,
               *, tq, tp, n_past):
    rows = ATT_HEADS * tq
    _stack_heads(q_ref[...], qbd_ref, tq)
    acc_ref[...] = jnp.zeros_like(acc_ref)
    carry_ref[...] = jnp.zeros_like(carry_ref)
    r = lax.broadcasted_iota(jnp.int32, (2 * LANES, 2 * LANES), 0) & (LANES - 1)
    c = lax.broadcasted_iota(jnp.int32, (2 * LANES, 2 * LANES), 1)
    wcum = jnp.where((c >= LANES) | (r > c), -1.0, 0.0).astype(BF16)

    def block(kt, vt, diagonal=False):
        kt = kt.astype(BF16)
        vt = vt.astype(BF16)
        tk = kt.shape[1]
        s = _dot(qbd_ref[...], kt)
        sp = _softplus2(s)
        log_beta = s - sp
        if diagonal:
            t_loc, s_loc = _local_positions(rows, tq, tk)
            valid = s_loc < t_loc
            sp = jnp.where(valid, sp, 0.0)
        hi = sp.astype(BF16)
        lo = (sp - hi.astype(F32)).astype(BF16)
        carry = carry_ref[...]
        after = [None] * (tk // LANES)
        for g in reversed(range(tk // LANES)):
            cols = slice(g * LANES, (g + 1) * LANES)
            a = _dot(jnp.concatenate([hi[:, cols], lo[:, cols]], axis=1), wcum)
            after[g] = a[:, :LANES] + carry
            carry = carry + a[:, LANES:]
        carry_ref[...] = carry
        w = jnp.exp2(log_beta + jnp.concatenate(after, axis=1))
        if diagonal:
            w = jnp.where(valid, w, 0.0)
        acc_ref[...] += _dot_nt(w.astype(BF16), vt)

    def any_weight_left():
        return (jnp.max(carry_ref[...]) > -SB_DEAD_LOG2).astype(jnp.int32)

    block(kd_ref[...], vd_ref[...], diagonal=True)
    n_blocks = pl.program_id(1) if n_past is None else n_past
    _past_loop(block, n_blocks, tp, (kp_ref, vp_ref), keep_going=any_weight_left)
    o_ref[...] = _unstack_heads(acc_ref[...], tq)


def _sb_attn(q, kd, vd, kp, vp, layer=None):
    b, n, _ = q.shape
    prompt = layer is None
    td = ATT_BLOCK
    tq = ATT_BLOCK if prompt else n
    rows = ATT_HEADS * tq
    q_spec, edge, past = _attn_specs(tq, td, prompt, layer)
    n_past = None if prompt else kp.shape[-1] // td
    return pl.pallas_call(
        functools.partial(_sb_kernel, tq=tq, tp=td, n_past=n_past),
        grid=(b, n // tq),
        in_specs=[q_spec, edge(ATT_WIDTH), edge(ATT_WIDTH), past(kp), past(vp)],
        out_specs=q_spec,
        out_shape=jax.ShapeDtypeStruct((b, n, ATT_WIDTH), F32),
        scratch_shapes=[pltpu.VMEM((rows, ATT_WIDTH), BF16), pltpu.VMEM((rows, ATT_WIDTH), F32),
                        pltpu.VMEM((rows, LANES), F32)],
        compiler_params=_params("parallel", "arbitrary"),
        name="sb_attn",
    )(q, kd, vd, kp, vp)


def _fox_kernel(q_ref, kd_ref, vd_ref, fd_ref, kp_ref, vp_ref, fp_ref, o_ref,
                qbd_ref, acc_ref, m_ref, l_ref, *, tq, td, tp, n_past):
    rows = ATT_HEADS * tq
    _stack_heads(q_ref[...], qbd_ref, tq)
    acc_ref[...] = jnp.zeros_like(acc_ref)
    m_ref[...] = jnp.full_like(m_ref, NEG)
    l_ref[...] = jnp.zeros_like(l_ref)

    def block(kt, vt, fk, diagonal=False):
        kt = kt.astype(BF16)
        vt = vt.astype(BF16)
        fk = fk * LOG2E
        tk = kt.shape[1]
        s = _dot(qbd_ref[...], kt)
        sh = jnp.concatenate([s[h * tq:(h + 1) * tq] - fk[h:h + 1, :] for h in range(ATT_HEADS)], axis=0)
        if diagonal:
            t_loc, s_loc = _local_positions(rows, tq, tk)
            sh = jnp.where(s_loc <= t_loc, sh, NEG)
        m_old = m_ref[...]
        m_new = jnp.maximum(m_old, jnp.max(sh, axis=1, keepdims=True))
        alpha = jnp.exp2(m_old - m_new)
        p = jnp.exp2(sh - _lane_tile(m_new, tk))
        l_ref[...] = alpha * l_ref[...] + jnp.sum(p, axis=1, keepdims=True)
        m_ref[...] = m_new
        acc_ref[...] = acc_ref[...] * _lane_tile(alpha, ATT_WIDTH) + _dot_nt(p.astype(BF16), vt)

    block(kd_ref[...], vd_ref[...], fd_ref[...], diagonal=True)
    past_refs = (kp_ref, vp_ref, fp_ref)
    if n_past is None:
        n_edge = pl.program_id(1)

        @pl.when(n_edge % 2 == 1)
        def _():
            start = pl.multiple_of((n_edge - 1) * td, td)
            block(*(r[:, pl.ds(start, td)] for r in past_refs))

        _past_loop(block, n_edge // 2, tp, past_refs)
    else:
        _past_loop(block, n_past, tp, past_refs)
    o_ref[...] = _unstack_heads(acc_ref[...] * _lane_tile(1.0 / l_ref[...], ATT_WIDTH), tq)


def _fox_attn(q, kd, vd, kp, vp, fcum, layer=None):
    b, n, _ = q.shape
    prompt = layer is None
    td = ATT_BLOCK
    tq = ATT_BLOCK if prompt else n
    tp = 2 * ATT_BLOCK if prompt else SAMPLE_PAST_BLOCK
    rows = ATT_HEADS * tq
    q_spec, edge, past = _attn_specs(tq, td, prompt, layer)
    n_past = None if prompt else kp.shape[-1] // tp
    fr = fcum.shape[1]
    if prompt:
        fd_spec = edge(fr)
    else:
        last = kp.shape[-1] // td
        fd_spec = pl.BlockSpec((None, fr, td), lambda i, j: (i, 0, last))
    fp_spec = pl.BlockSpec((None,) + fcum.shape[1:], lambda i, j: (i, 0, 0))
    return pl.pallas_call(
        functools.partial(_fox_kernel, tq=tq, td=td, tp=tp, n_past=n_past),
        grid=(b, n // tq),
        in_specs=[q_spec, edge(ATT_WIDTH), edge(ATT_WIDTH), fd_spec, past(kp), past(vp), fp_spec],
        out_specs=q_spec,
        out_shape=jax.ShapeDtypeStruct((b, n, ATT_WIDTH), F32),
        scratch_shapes=[pltpu.VMEM((rows, ATT_WIDTH), BF16), pltpu.VMEM((rows, ATT_WIDTH), F32),
                        pltpu.VMEM((rows, LANES), F32), pltpu.VMEM((rows, LANES), F32)],
        compiler_params=_params("parallel", "arbitrary"),
        name="fox_attn",
    )(q, kd, vd, fcum, kp, vp, fcum)


def _ssd_kernel(xbc_ref, z_ref, sm_ref, smt_ref, cinit_ref, h0_ref, cw_ref, cb_ref, alane_ref, acol_ref,
                dskip_ref, g_ref, y_ref, hfin_ref, xp_ref, h_ref, *, L):
    @pl.when(pl.program_id(1) == 0)
    def _():
        h_ref[...] = h0_ref[0]
        xp_ref[0:CONV_PAD, :] = cinit_ref[0]

    xp_ref[CONV_PAD:CONV_PAD + L, :] = xbc_ref[0]
    first = CONV_PAD - (SSD_CONV - 1)
    conv = cb_ref[...]
    for i in range(SSD_CONV):
        conv = conv + cw_ref[i:i + 1, :] * xp_ref[first + i:first + i + L, :]
    xp_ref[0:CONV_PAD, :] = xp_ref[L:L + CONV_PAD, :]
    xc = _silu(conv)
    xs = xc[:, :SSD_INNER]

    dt_col = sm_ref[0]
    a_col = dt_col * alane_ref[...]
    ri = lax.broadcasted_iota(jnp.int32, (L, L), 0)
    ci = lax.broadcasted_iota(jnp.int32, (L, L), 1)
    causal = ci <= ri
    tri_lower = jnp.where(causal, 1.0, 0.0).astype(BF16)
    tri_upper = jnp.where(ri <= ci, 1.0, 0.0).astype(BF16)
    acum_col = sum(_dot(tri_lower, part) for part in _split3(a_col))
    a_row = smt_ref[0] * acol_ref[...]
    acum_row = _dot(jnp.concatenate(_split3(a_row), axis=1),
                    jnp.concatenate([tri_upper] * 3, axis=0))
    a_last = acum_col[L - 1:L, :]

    ys = []
    heads_per_group = SSD_HEADS // SSD_GROUPS
    for grp in range(SSD_GROUPS):
        b0 = SSD_INNER + grp * SSD_STATE
        c0 = SSD_INNER + SSD_GROUPS * SSD_STATE + grp * SSD_STATE
        bg = xc[:, b0:b0 + SSD_STATE].astype(BF16)
        cg = xc[:, c0:c0 + SSD_STATE].astype(BF16)
        cb = _dot_nt(cg, bg)
        for hh in range(heads_per_group):
            h = grp * heads_per_group + hh
            col = SMALL_DT + h
            ac = acum_col[:, col:col + 1]
            ar = acum_row[col:col + 1, :]
            decay = jnp.exp(jnp.where(causal, ac - ar, NEG))
            scores = (cb * decay).astype(BF16)
            xdt = xs[:, h * SSD_HEAD_DIM:(h + 1) * SSD_HEAD_DIM] * dt_col[:, col:col + 1]
            state = h_ref[h]
            y_h = _dot(scores, xdt.astype(BF16)) + _dot_nt(cg, state.astype(BF16)) * jnp.exp(ac)
            al = a_last[:, col:col + 1]
            xw = (xdt * jnp.exp(al - ac)).astype(BF16)
            h_ref[h] = jnp.exp(al) * state + lax.dot_general(xw, bg, TN_DIMS, preferred_element_type=F32)
            ys.append(y_h)
    y = jnp.concatenate(ys, axis=1) + dskip_ref[...] * xs
    y_ref[0] = _rms(y * _silu(z_ref[0]), g_ref[...])
    hfin_ref[0] = h_ref[...]


def _ssd(xbc, z, small, small_t, conv_init, h0, p, layer, L):
    b, n, _ = xbc.shape
    weights = [p["conv_w"], p["conv_b"], p["a_lane"], p["a_col"], p["dskip_vec"], p["ssd_norm_g"]]
    seq = lambda w: pl.BlockSpec((1, L, w), lambda i, j: (i, j, 0))
    per_b = lambda *s: pl.BlockSpec((1,) + s, lambda i, j: (i,) + (0,) * len(s))
    return pl.pallas_call(
        functools.partial(_ssd_kernel, L=L),
        grid=(b, n // L),
        in_specs=[seq(CONV_DIM), seq(SSD_INNER), seq(LANES),
                  pl.BlockSpec((1, small_t.shape[1], L), lambda i, j: (i, 0, j)),
                  per_b(CONV_PAD, CONV_DIM), per_b(SSD_HEADS, SSD_HEAD_DIM, SSD_STATE)]
        + [_wspec(w, layer) for w in weights],
        out_specs=[seq(SSD_INNER), per_b(SSD_HEADS, SSD_HEAD_DIM, SSD_STATE)],
        out_shape=[jax.ShapeDtypeStruct((b, n, SSD_INNER), F32),
                   jax.ShapeDtypeStruct((b, SSD_HEADS, SSD_HEAD_DIM, SSD_STATE), F32)],
        scratch_shapes=[pltpu.VMEM((L + CONV_PAD, CONV_DIM), F32),
                        pltpu.VMEM((SSD_HEADS, SSD_HEAD_DIM, SSD_STATE), F32)],
        compiler_params=_params("parallel", "arbitrary"),
        name="ssd",
    )(xbc, z, small, small_t, conv_init, h0, *weights)


def _post_mix_kernel(h_ref, sb_ref, fx_ref, ssd_ref, wout_ref, gx_ref, wq_ref, mk_ref, mv_ref, wo_ref, o_ref,
                     h1_ref, q_ref, att_ref, *, rows_per_batch):
    j = pl.program_id(1)
    d = h_ref.shape[1]
    xd = d // X_HEADS

    @pl.when(j == 0)
    def _():
        mix = (_dot(sb_ref[...].astype(BF16), wout_ref[0:ATT_WIDTH, :])
               + _dot(fx_ref[...].astype(BF16), wout_ref[ATT_WIDTH:2 * ATT_WIDTH, :])
               + _dot(ssd_ref[...].astype(BF16), wout_ref[2 * ATT_WIDTH:, :]))
        h1 = h_ref[...] + mix
        h1_ref[...] = h1
        u = _rms(h1, gx_ref[...]).astype(BF16)
        q_ref[...] = (_dot(u, wq_ref[...]) * (LOG2E * xd ** -0.5)).astype(BF16)

    rows = pl.ds(pl.multiple_of(j * rows_per_batch, rows_per_batch), rows_per_batch)
    for hd in range(X_HEADS):
        cols = slice(hd * xd, (hd + 1) * xd)
        s = _dot_nt(q_ref[rows, cols], mk_ref[:, hd, :].astype(BF16))
        p = jnp.exp2(s - jnp.max(s, axis=1, keepdims=True))
        o = _dot(p.astype(BF16), mv_ref[:, hd, :].astype(BF16))
        att_ref[rows, cols] = (o * (1.0 / jnp.sum(p, axis=1, keepdims=True))).astype(BF16)

    @pl.when(j == pl.num_programs(1) - 1)
    def _():
        o_ref[...] = h1_ref[...] + _dot(att_ref[...], wo_ref[...])


def _post_mix(h, sb_o, fox_o, ssd_o, p, layer, mk, mv, mem_layer, n, tm):
    m, d = h.shape
    rows_per_batch = min(tm, n)
    row = lambda w: pl.BlockSpec((tm, w), lambda i, j: (i, 0))
    mem_block = mk.shape[-3:]
    if mem_layer is None:
        mem_spec = pl.BlockSpec((None,) + mem_block, lambda i, j: ((i * tm) // n + j, 0, 0, 0))
    else:
        mem_spec = pl.BlockSpec((None, None) + mem_block, lambda i, j: (mem_layer, (i * tm) // n + j, 0, 0, 0))
    w_out, gx, wq, wo = p["w_out"], p["norm_x_g"], p["wq_x"], p["wo_x"]
    return pl.pallas_call(
        functools.partial(_post_mix_kernel, rows_per_batch=rows_per_batch),
        grid=(m // tm, tm // rows_per_batch),
        in_specs=[row(d), row(ATT_WIDTH), row(ATT_WIDTH), row(SSD_INNER), _wspec(w_out, layer),
                  _wspec(gx, layer), _wspec(wq, layer), mem_spec, mem_spec, _wspec(wo, layer)],
        out_specs=row(d),
        out_shape=jax.ShapeDtypeStruct(h.shape, F32),
        scratch_shapes=[pltpu.VMEM((tm, d), F32), pltpu.VMEM((tm, d), BF16), pltpu.VMEM((tm, d), BF16)],
        compiler_params=_params("parallel", "arbitrary"),
        name="post_mix",
    )(h, sb_o, fox_o, ssd_o, w_out, gx, wq, mk, mv, wo)


def _ffn_kernel(h_ref, g_ref, wg_ref, wu_ref, wd_ref, gf_ref, o_ref, *, final_norm):
    h = h_ref[...]
    u = _rms(h, g_ref[...]).astype(BF16)
    hidden = (_silu(_dot(u, wg_ref[...])) * _dot(u, wu_ref[...])).astype(BF16)
    out = h + _dot(hidden, wd_ref[...])
    if final_norm:
        out = _rms(out, gf_ref[...])
    o_ref[...] = out


def _ffn(h, p, layer, g_final, final_norm, tm):
    m, d = h.shape
    row = pl.BlockSpec((tm, d), lambda i: (i, 0))
    weights = [p["norm_ffn_g"], p["w_gate"], p["w_up"], p["w_down"]]
    return pl.pallas_call(
        functools.partial(_ffn_kernel, final_norm=final_norm),
        grid=(m // tm,),
        in_specs=[row] + [_wspec(w, layer) for w in weights] + [_wspec(g_final)],
        out_specs=row,
        out_shape=jax.ShapeDtypeStruct(h.shape, F32),
        compiler_params=_params("parallel"),
        name="ffn",
    )(h, *weights, g_final)


def _mem_kv_kernel(x_ref, g_ref, wk_ref, wv_ref, k_ref, v_ref):
    u = _rms(x_ref[...], g_ref[...]).astype(BF16)
    xd = k_ref.shape[2]
    for w_ref, o_ref in ((wk_ref, k_ref), (wv_ref, v_ref)):
        y = _dot(u, w_ref[...])
        for hd in range(X_HEADS):
            o_ref[:, hd, :] = y[:, hd * xd:(hd + 1) * xd]


def _mem_kv(mem, p, layer):
    b, t, d = mem.shape
    out_block = (t, X_HEADS, d // X_HEADS)
    out_spec = pl.BlockSpec((None,) + out_block, lambda i: (i, 0, 0, 0))
    weights = [p["mem_norm_g"], p["wk_x"], p["wv_x"]]
    return pl.pallas_call(
        _mem_kv_kernel,
        grid=(b,),
        in_specs=[pl.BlockSpec((None, t, d), lambda i: (i, 0, 0))] + [_wspec(w, layer) for w in weights],
        out_specs=[out_spec, out_spec],
        out_shape=[jax.ShapeDtypeStruct((b,) + out_block, F32)] * 2,
        compiler_params=_params("parallel"),
        name="mem_kv",
    )(mem, *weights)


def _seq_minor(a):
    layers, b, n = a.shape[:3]
    return jnp.transpose(a, (0, 1, 3, 4, 2)).reshape(layers, b, -1, n)


def _layer(h, p, layer, past, mem_k, mem_v, final_g, final_norm):
    b, n, d = h.shape
    m = b * n
    prompt = past is None
    tm = min(m, 512)
    outs = _in_proj(h.reshape(m, d), p, layer, tm, b if prompt else None)
    sq, fq, z, xbc, small = (a.reshape(b, n, a.shape[-1]) for a in outs[:5])
    small_t = outs[5]
    kv_t = outs[6:]
    if not prompt:
        per_batch = lambda a: jnp.transpose(a.reshape(a.shape[0], b, n), (1, 0, 2))
        small_t = per_batch(small_t)
        kv_t = [per_batch(a) for a in kv_t]
    skt, svt, fkt, fvt, skb, svb, fkb, fvb = kv_t
    logf_t = small_t[:, SMALL_LOGF:SMALL_LOGF + ATT_HEADS, :]

    if prompt:
        fcum = _seq_cumsum(small_t[:, :CONV_PAD, :])
        sb_o = _sb_attn(sq, skb, svb, skb, svb)
        fox_o = _fox_attn(fq, fkb, fvb, fkb, fvb, fcum)
        conv_init = jnp.zeros((b, CONV_PAD, CONV_DIM), F32)
        h0 = jnp.zeros((b, SSD_HEADS, SSD_HEAD_DIM, SSD_STATE), F32)
        conv_hist = None
        ssd_len = ATT_BLOCK
    else:
        pad_keys = lambda a: jnp.pad(a, ((0, 0), (0, 0), (0, ATT_BLOCK - n)))
        sb_o = _sb_attn(sq, pad_keys(skb), pad_keys(svb), past["sb_k"], past["sb_v"], layer)
        logf_all = jnp.concatenate([jnp.transpose(past["fox_logf"], (0, 2, 1)), logf_t], axis=2)
        logf_all = jnp.pad(logf_all, ((0, 0), (0, CONV_PAD - ATT_HEADS), (0, ATT_BLOCK - n)))
        fcum = _seq_cumsum(logf_all)
        fox_o = _fox_attn(fq, pad_keys(fkb), pad_keys(fvb), past["fox_k"], past["fox_v"], fcum, layer)
        conv_hist = past["conv"]
        conv_init = jnp.pad(conv_hist, ((0, 0), (CONV_PAD - (SSD_CONV - 1), 0), (0, 0)))
        h0 = past["ssm"]
        ssd_len = n

    ssd_o, h_final = _ssd(xbc, z, small, small_t, conv_init, h0, p, layer, ssd_len)
    if conv_hist is None:
        conv_state = xbc[:, n - (SSD_CONV - 1):, :]
    else:
        conv_state = jnp.concatenate([conv_hist, xbc], axis=1)[:, -(SSD_CONV - 1):, :]

    flat = lambda a: a.reshape(m, a.shape[-1])
    h1 = _post_mix(flat(h), flat(sb_o), flat(fox_o), flat(ssd_o), p, layer, mem_k, mem_v,
                   None if prompt else layer, n, tm)
    h2 = _ffn(h1, p, layer, final_g, final_norm, tm)
    return h2.reshape(b, n, d), (skt, svt, fkt, fvt, logf_t, h_final, conv_state)


def _stacked_params(norm_mix_g, w_in, fox_b_f, conv_w, conv_b, dt_bias, a_log, d_skip, ssd_norm_g, w_out,
                    norm_x_g, mem_norm_g, wq_x, wk_x, wv_x, wo_x, norm_ffn_g, w_gate, w_up, w_down):
    aw = ATT_WIDTH
    qkv = 6 * aw
    gate_end = qkv + ATT_HEADS
    z_end = gate_end + SSD_INNER
    xbc_end = z_end + CONV_DIM
    w = jnp.transpose(w_in, (0, 2, 1))
    layers, _, d = w.shape
    row = lambda a: a.reshape(layers, 1, -1).astype(F32)
    small_pad = LANES - (ATT_HEADS + SSD_HEADS)
    zeros = lambda n: jnp.zeros((layers, n), F32)
    a_small = jnp.concatenate([zeros(ATT_HEADS), -jnp.exp(a_log.astype(F32)), zeros(small_pad)], axis=1)
    b_small = jnp.concatenate([fox_b_f.astype(F32), dt_bias.astype(F32), zeros(small_pad)], axis=1)
    col = lambda a: a[:, :2 * CONV_PAD].reshape(layers, 2 * CONV_PAD, 1)
    bf = lambda a: a.astype(BF16)
    return dict(
        norm_mix_g=row(norm_mix_g),
        wq_t=bf(jnp.concatenate([w[:, 0:aw], w[:, 3 * aw:4 * aw]], axis=1)),
        wkv_t=bf(jnp.concatenate([w[:, aw:3 * aw], w[:, 4 * aw:6 * aw]], axis=1)),
        wzx_t=bf(w[:, gate_end:xbc_end]),
        ws_t=bf(jnp.concatenate([w[:, qkv:gate_end], w[:, xbc_end:], jnp.zeros((layers, small_pad, d), w.dtype)],
                                axis=1)),
        b_small_row=row(b_small), b_small_col=col(b_small),
        conv_w=conv_w.astype(F32), conv_b=row(conv_b),
        a_lane=row(a_small), a_col=col(a_small),
        dskip_vec=row(jnp.repeat(d_skip, SSD_HEAD_DIM, axis=1)), ssd_norm_g=row(ssd_norm_g),
        w_out=bf(w_out), norm_x_g=row(norm_x_g), mem_norm_g=row(mem_norm_g),
        wq_x=bf(wq_x), wk_x=bf(wk_x), wv_x=bf(wv_x), wo_x=bf(wo_x), norm_ffn_g=row(norm_ffn_g),
        w_gate=bf(w_gate), w_up=bf(w_up), w_down=bf(w_down))


def kernel(x_prompt, x_sample, cache_sb_k, cache_sb_v, cache_fox_k, cache_fox_v, cache_fox_logf, state_ssm, state_conv, cache_mem_k, cache_mem_v, mem_prompt, norm_mix_g, w_in, fox_b_f, conv_w, conv_b, dt_bias, a_log, d_skip, ssd_norm_g, w_out, norm_x_g, mem_norm_g, wq_x, wk_x, wv_x, wo_x, norm_ffn_g, w_gate, w_up, w_down, final_norm_g):
    depth = w_in.shape[0]
    bp, mem_len, d = mem_prompt.shape
    bs = x_sample.shape[0]
    final_g = final_norm_g.reshape(1, d).astype(F32)
    p = _stacked_params(norm_mix_g, w_in, fox_b_f, conv_w, conv_b, dt_bias, a_log, d_skip, ssd_norm_g, w_out,
                        norm_x_g, mem_norm_g, wq_x, wk_x, wv_x, wo_x, norm_ffn_g, w_gate, w_up, w_down)
    caches_t = [_seq_minor(c) for c in (cache_sb_k, cache_sb_v, cache_fox_k, cache_fox_v)]
    hp, hs = x_prompt, x_sample
    p_st, s_st, p_mk, p_mv = [], [], [], []
    for l in range(depth):
        last = l == depth - 1
        mk, mv = _mem_kv(mem_prompt, p, l)
        hp, st = _layer(hp, p, l, None, mk, mv, final_g, last)
        p_st.append(st)
        p_mk.append(mk)
        p_mv.append(mv)
        past = dict(sb_k=caches_t[0], sb_v=caches_t[1], fox_k=caches_t[2], fox_v=caches_t[3],
                    fox_logf=cache_fox_logf[l], ssm=state_ssm[l], conv=state_conv[l])
        hs, st = _layer(hs, p, l, past, cache_mem_k, cache_mem_v, final_g, last)
        s_st.append(st)
    stk = lambda sts, i: jnp.stack([s[i] for s in sts], axis=0)
    kv = lambda sts, i: jnp.transpose(
        stk(sts, i).reshape(depth, sts[0][i].shape[0], ATT_HEADS, HEAD_DIM, -1), (0, 1, 4, 2, 3))
    logf = lambda sts: jnp.transpose(stk(sts, 4), (0, 1, 3, 2))
    return (hp, hs,
            kv(p_st, 0), kv(p_st, 1), kv(p_st, 2), kv(p_st, 3), logf(p_st), stk(p_st, 5), stk(p_st, 6),
            jnp.stack(p_mk, axis=0), jnp.stack(p_mv, axis=0),
            kv(s_st, 0), kv(s_st, 1), kv(s_st, 2), kv(s_st, 3), logf(s_st), stk(s_st, 5), stk(s_st, 6))
```

```python
import functools

import jax
import jax.numpy as jnp
from jax import lax
from jax.experimental import pallas as pl
from jax.experimental.pallas import tpu as pltpu

F32 = jnp.float32
BF16 = jnp.bfloat16
EPS = 1e-6

HEAD_DIM = 64
ATT_HEADS = 4
ATT_WIDTH = ATT_HEADS * HEAD_DIM
SSD_HEADS = 8
SSD_HEAD_DIM = 64
SSD_INNER = SSD_HEADS * SSD_HEAD_DIM
SSD_GROUPS = 2
SSD_STATE = 128
SSD_CONV = 4
CONV_DIM = SSD_INNER + 2 * SSD_GROUPS * SSD_STATE
X_HEADS = 4
LANES = 128
SMALL_LOGF = 0
SMALL_DT = ATT_HEADS
CONV_PAD = 8
ATT_BLOCK = 256
SAMPLE_PAST_BLOCK = 1024
NEG = -1e30
LOG2E = 1.4426950408889634
SB_DEAD_LOG2 = 150.0
VMEM_LIMIT = 56 * 1024 * 1024

NT_DIMS = (((1,), (1,)), ((), ()))
TN_DIMS = (((0,), (0,)), ((), ()))


def _params(*sem):
    return pltpu.CompilerParams(dimension_semantics=sem, vmem_limit_bytes=VMEM_LIMIT)


def _rms(x, g):
    ms = jnp.mean(x * x, axis=-1, keepdims=True)
    return x * lax.rsqrt(ms + EPS) * g


def _softplus(x):
    t = jnp.exp2(jnp.abs(x) * (-LOG2E))
    return jnp.maximum(x, 0.0) + jnp.log(1.0 + t)


def _softplus2(x):
    return jnp.maximum(x, 0.0) + jnp.log2(1.0 + jnp.exp2(-jnp.abs(x)))


def _silu(x):
    half = 0.5 * x
    return half + half * jnp.tanh(half)


def _dot(a, b):
    return jnp.dot(a, b, preferred_element_type=F32)


def _dot_nt(a, b):
    return lax.dot_general(a, b, NT_DIMS, preferred_element_type=F32)


def _split3(x):
    hi = x.astype(BF16)
    r1 = x - hi.astype(F32)
    mid = r1.astype(BF16)
    lo = (r1 - mid.astype(F32)).astype(BF16)
    return hi, mid, lo


def _wspec(a, layer=None):
    if layer is None:
        return pl.BlockSpec(a.shape, lambda *_: (0,) * a.ndim)
    return pl.BlockSpec((None,) + a.shape[1:], lambda *_: (layer,) + (0,) * (a.ndim - 1))


def _in_proj_kernel(*refs, n_aliased):
    x_ref, g_ref, wq_ref, wkv_ref, wzx_ref, ws_ref, bs_row_ref, bs_col_ref = refs[:8]
    (sq_ref, fq_ref, z_ref, xbc_ref, small_ref, small_t_ref,
     skt_ref, svt_ref, fkt_ref, fvt_ref, skb_ref, svb_ref, fkb_ref, fvb_ref) = refs[8 + n_aliased:]
    u = _rms(x_ref[...], g_ref[...]).astype(BF16)
    q = _dot_nt(u, wq_ref[...]) * (LOG2E * HEAD_DIM ** -0.5)
    sq_ref[...] = q[:, :ATT_WIDTH].astype(BF16)
    fq_ref[...] = q[:, ATT_WIDTH:].astype(BF16)
    zx = _dot_nt(u, wzx_ref[...])
    z_ref[...] = zx[:, :SSD_INNER]
    xbc_ref[...] = zx[:, SSD_INNER:]
    kvt = _dot_nt(wkv_ref[...], u)
    for i, (f_ref, b_ref) in enumerate(((skt_ref, skb_ref), (svt_ref, svb_ref),
                                        (fkt_ref, fkb_ref), (fvt_ref, fvb_ref))):
        part = kvt[i * ATT_WIDTH:(i + 1) * ATT_WIDTH, :]
        f_ref[...] = part.reshape(f_ref.shape)
        b_ref[...] = part.astype(BF16).reshape(b_ref.shape)

    def gates(s, index):
        is_gate = index < SMALL_DT
        sp = _softplus(jnp.where(is_gate, -s, s))
        return jnp.where(is_gate, -sp, sp)

    s = _dot_nt(u, ws_ref[...]) + bs_row_ref[...]
    small_ref[...] = gates(s, lax.broadcasted_iota(jnp.int32, s.shape, 1))
    st = _dot_nt(ws_ref[0:2 * CONV_PAD, :], u) + bs_col_ref[...]
    small_t_ref[...] = gates(st, lax.broadcasted_iota(jnp.int32, st.shape, 0)).reshape(small_t_ref.shape)


def _in_proj(x, p, layer, tm, batch, stacked=None):
    m, d = x.shape
    row = lambda n: pl.BlockSpec((tm, n), lambda i: (i, 0))
    weights = [p["norm_mix_g"], p["wq_t"], p["wkv_t"], p["wzx_t"], p["ws_t"], p["b_small_row"], p["b_small_col"]]
    rows = [(ATT_WIDTH, BF16), (ATT_WIDTH, BF16), (SSD_INNER, F32), (CONV_DIM, F32), (LANES, F32)]
    if batch is None:
        t_shape = kv_shape = lambda r: (r, m)
        t_spec = kv_spec = lambda r: pl.BlockSpec((r, tm), lambda i: (0, i))
    else:
        nb = m // batch // tm
        layers = weights[0].shape[0]
        t_shape = lambda r: (batch, r, m // batch)
        t_spec = lambda r: pl.BlockSpec((1, r, tm), lambda i: (i // nb, 0, i % nb))
        kv_shape = lambda r: (layers, batch, r, m // batch)
        kv_spec = lambda r: pl.BlockSpec((None, None, r, tm), lambda i: (layer, i // nb, 0, i % nb))
    out_specs = ([row(n) for n, _ in rows] + [t_spec(2 * CONV_PAD)] + [kv_spec(ATT_WIDTH)] * 4
                 + [t_spec(ATT_WIDTH)] * 4)
    out_shape = ([jax.ShapeDtypeStruct((m, n), dt) for n, dt in rows]
                 + [jax.ShapeDtypeStruct(t_shape(2 * CONV_PAD), F32)]
                 + [jax.ShapeDtypeStruct(kv_shape(ATT_WIDTH), F32)] * 4
                 + [jax.ShapeDtypeStruct(t_shape(ATT_WIDTH), BF16)] * 4)
    stacked = list(stacked or [])
    first_kv_out = len(rows) + 1
    first_alias_in = 1 + len(weights)
    return pl.pallas_call(
        functools.partial(_in_proj_kernel, n_aliased=len(stacked)),
        grid=(m // tm,),
        in_specs=[row(d)] + [_wspec(w, layer) for w in weights]
        + [pl.BlockSpec(memory_space=pl.ANY)] * len(stacked),
        out_specs=out_specs,
        out_shape=out_shape,
        input_output_aliases={first_alias_in + i: first_kv_out + i for i in range(len(stacked))},
        compiler_params=_params("parallel"),
        name="in_proj",
    )(x, *weights, *stacked)


def _cumsum_kernel(x_ref, o_ref, *, blk):
    n = x_ref.shape[2]
    r = lax.broadcasted_iota(jnp.int32, (blk, blk), 0)
    c = lax.broadcasted_iota(jnp.int32, (blk, blk), 1)
    tri = jnp.where(r <= c, 1.0, 0.0).astype(BF16)
    tri3 = jnp.concatenate([tri, tri, tri], axis=0)
    carry = jnp.zeros((x_ref.shape[1], 1), F32)
    for j in range(n // blk):
        xb = x_ref[0, :, j * blk:(j + 1) * blk]
        out = _dot(jnp.concatenate(_split3(xb), axis=1), tri3) + carry
        o_ref[0, :, j * blk:(j + 1) * blk] = out
        carry = out[:, blk - 1:blk]


def _seq_cumsum(x):
    b, r, n = x.shape
    spec = pl.BlockSpec((1, b * r, n), lambda i: (0, 0, 0))
    return pl.pallas_call(
        functools.partial(_cumsum_kernel, blk=ATT_BLOCK),
        grid=(1,),
        in_specs=[spec],
        out_specs=spec,
        out_shape=jax.ShapeDtypeStruct((1, b * r, n), F32),
        compiler_params=_params("arbitrary"),
        name="seq_cumsum",
    )(x.reshape(1, b * r, n)).reshape(b, r, n)


def _stack_heads(q, qbd_ref, tq):
    lane_head = lax.broadcasted_iota(jnp.int32, q.shape, 1) // HEAD_DIM
    for h in range(ATT_HEADS):
        qbd_ref[h * tq:(h + 1) * tq, :] = jnp.where(lane_head == h, q, jnp.zeros_like(q))


def _unstack_heads(acc, tq):
    lane_head = lax.broadcasted_iota(jnp.int32, (tq, ATT_WIDTH), 1) // HEAD_DIM
    out = jnp.zeros((tq, ATT_WIDTH), F32)
    for h in range(ATT_HEADS):
        out = jnp.where(lane_head == h, acc[h * tq:(h + 1) * tq, :], out)
    return out


def _local_positions(rows, tq, tk):
    t_loc = lax.broadcasted_iota(jnp.int32, (rows, tk), 0) & (tq - 1)
    s_loc = lax.broadcasted_iota(jnp.int32, (rows, tk), 1)
    return t_loc, s_loc


def _lane_tile(x, width):
    return jnp.concatenate([x] * (width // LANES), axis=1)


def _past_loop(block, n_blocks, tp, refs, keep_going=None):
    def run(step):
        start = pl.multiple_of((n_blocks - 1 - step) * tp, tp)
        block(*(r[:, pl.ds(start, tp)] for r in refs))

    if keep_going is None:
        def body(step, carry):
            run(step)
            return carry

        lax.fori_loop(0, n_blocks, body, 0)
    else:
        def body(state):
            run(state[0])
            return state[0] + 1, keep_going()

        lax.while_loop(lambda state: (state[0] < n_blocks) & (state[1] > 0), body,
                       (jnp.int32(0), keep_going()))


def _attn_specs(tq, td, prompt, layer):
    q_spec = pl.BlockSpec((None, tq, ATT_WIDTH), lambda i, j: (i, j, 0))
    if prompt:
        edge = lambda r: pl.BlockSpec((None, r, td), lambda i, j: (i, 0, j))
        past = lambda a: pl.BlockSpec((None,) + a.shape[1:], lambda i, j: (i, 0, 0))
    else:
        edge = lambda r: pl.BlockSpec((None, r, td), lambda i, j: (i, 0, 0))
        past = lambda a: pl.BlockSpec((None, None) + a.shape[2:], lambda i, j: (layer, i, 0, 0))
    return q_spec, edge, past


def _sb_kernel(q_ref, kd_ref, vd_ref, kp_ref, vp_ref, o_ref, qbd_ref, acc_ref, carry_ref,
               *, tq, tp, n_past):
    rows = ATT_HEADS * tq
    _stack_heads(q_ref[...], qbd_ref, tq)
    acc_ref[...] = jnp.zeros_like(acc_ref)
    carry_ref[...] = jnp.zeros_like(carry_ref)
    r = lax.broadcasted_iota(jnp.int32, (2 * LANES, 2 * LANES), 0) & (LANES - 1)
    c = lax.broadcasted_iota(jnp.int32, (2 * LANES, 2 * LANES), 1)
    wcum = jnp.where((c >= LANES) | (r > c), -1.0, 0.0).astype(BF16)

    def block(kt, vt, diagonal=False):
        kt = kt.astype(BF16)
        vt = vt.astype(BF16)
        tk = kt.shape[1]
        s = _dot(qbd_ref[...], kt)
        sp = _softplus2(s)
        log_beta = s - sp
        if diagonal:
            t_loc, s_loc = _local_positions(rows, tq, tk)
            valid = s_loc < t_loc
            sp = jnp.where(valid, sp, 0.0)
        hi = sp.astype(BF16)
        lo = (sp - hi.astype(F32)).astype(BF16)
        carry = carry_ref[...]
        after = [None] * (tk // LANES)
        for g in reversed(range(tk // LANES)):
            cols = slice(g * LANES, (g + 1) * LANES)
            a = _dot(jnp.concatenate([hi[:, cols], lo[:, cols]], axis=1), wcum)
            after[g] = a[:, :LANES] + carry
            carry = carry + a[:, LANES:]
        carry_ref[...] = carry
        w = jnp.exp2(log_beta + jnp.concatenate(after, axis=1))
        if diagonal:
            w = jnp.where(valid, w, 0.0)
        acc_ref[...] += _dot_nt(w.astype(BF16), vt)

    def any_weight_left():
        return (jnp.max(carry_ref[...]) > -SB_DEAD_LOG2).astype(jnp.int32)

    block(kd_ref[...], vd_ref[...], diagonal=True)
    n_blocks = pl.program_id(1) if n_past is None else n_past
    _past_loop(block, n_blocks, tp, (kp_ref, vp_ref), keep_going=any_weight_left)
    o_ref[...] = _unstack_heads(acc_ref[...], tq)


def _sb_attn(q, kd, vd, kp, vp, layer=None):
    b, n, _ = q.shape
    prompt = layer is None
    td = ATT_BLOCK
    tq = ATT_BLOCK if prompt else n
    rows = ATT_HEADS * tq
    q_spec, edge, past = _attn_specs(tq, td, prompt, layer)
    n_past = None if prompt else kp.shape[-1] // td
    return pl.pallas_call(
        functools.partial(_sb_kernel, tq=tq, tp=td, n_past=n_past),
        grid=(b, n // tq),
        in_specs=[q_spec, edge(ATT_WIDTH), edge(ATT_WIDTH), past(kp), past(vp)],
        out_specs=q_spec,
        out_shape=jax.ShapeDtypeStruct((b, n, ATT_WIDTH), F32),
        scratch_shapes=[pltpu.VMEM((rows, ATT_WIDTH), BF16), pltpu.VMEM((rows, ATT_WIDTH), F32),
                        pltpu.VMEM((rows, LANES), F32)],
        compiler_params=_params("parallel", "arbitrary"),
        name="sb_attn",
    )(q, kd, vd, kp, vp)


def _fox_kernel(q_ref, kd_ref, vd_ref, fd_ref, kp_ref, vp_ref, fp_ref, o_ref,
                qbd_ref, acc_ref, m_ref, l_ref, *, tq, td, tp, n_past):
    rows = ATT_HEADS * tq
    _stack_heads(q_ref[...], qbd_ref, tq)
    acc_ref[...] = jnp.zeros_like(acc_ref)
    m_ref[...] = jnp.full_like(m_ref, NEG)
    l_ref[...] = jnp.zeros_like(l_ref)

    def block(kt, vt, fk, diagonal=False):
        kt = kt.astype(BF16)
        vt = vt.astype(BF16)
        fk = fk * LOG2E
        tk = kt.shape[1]
        s = _dot(qbd_ref[...], kt)
        sh = jnp.concatenate([s[h * tq:(h + 1) * tq] - fk[h:h + 1, :] for h in range(ATT_HEADS)], axis=0)
        if diagonal:
            t_loc, s_loc = _local_positions(rows, tq, tk)
            sh = jnp.where(s_loc <= t_loc, sh, NEG)
        m_old = m_ref[...]
        m_new = jnp.maximum(m_old, jnp.max(sh, axis=1, keepdims=True))
        alpha = jnp.exp2(m_old - m_new)
        p = jnp.exp2(sh - _lane_tile(m_new, tk))
        l_ref[...] = alpha * l_ref[...] + jnp.sum(p, axis=1, keepdims=True)
        m_ref[...] = m_new
        acc_ref[...] = acc_ref[...] * _lane_tile(alpha, ATT_WIDTH) + _dot_nt(p.astype(BF16), vt)

    block(kd_ref[...], vd_ref[...], fd_ref[...], diagonal=True)
    past_refs = (kp_ref, vp_ref, fp_ref)
    if n_past is None:
        n_edge = pl.program_id(1)

        @pl.when(n_edge % 2 == 1)
        def _():
            start = pl.multiple_of((n_edge - 1) * td, td)
            block(*(r[:, pl.ds(start, td)] for r in past_refs))

        _past_loop(block, n_edge // 2, tp, past_refs)
    else:
        _past_loop(block, n_past, tp, past_refs)
    o_ref[...] = _unstack_heads(acc_ref[...] * _lane_tile(1.0 / l_ref[...], ATT_WIDTH), tq)


def _fox_attn(q, kd, vd, kp, vp, fcum, layer=None):
    b, n, _ = q.shape
    prompt = layer is None
    td = ATT_BLOCK
    tq = ATT_BLOCK if prompt else n
    tp = 2 * ATT_BLOCK if prompt else SAMPLE_PAST_BLOCK
    rows = ATT_HEADS * tq
    q_spec, edge, past = _attn_specs(tq, td, prompt, layer)
    n_past = None if prompt else kp.shape[-1] // tp
    fr = fcum.shape[1]
    if prompt:
        fd_spec = edge(fr)
    else:
        last = kp.shape[-1] // td
        fd_spec = pl.BlockSpec((None, fr, td), lambda i, j: (i, 0, last))
    fp_spec = pl.BlockSpec((None,) + fcum.shape[1:], lambda i, j: (i, 0, 0))
    return pl.pallas_call(
        functools.partial(_fox_kernel, tq=tq, td=td, tp=tp, n_past=n_past),
        grid=(b, n // tq),
        in_specs=[q_spec, edge(ATT_WIDTH), edge(ATT_WIDTH), fd_spec, past(kp), past(vp), fp_spec],
        out_specs=q_spec,
        out_shape=jax.ShapeDtypeStruct((b, n, ATT_WIDTH), F32),
        scratch_shapes=[pltpu.VMEM((rows, ATT_WIDTH), BF16), pltpu.VMEM((rows, ATT_WIDTH), F32),
                        pltpu.VMEM((rows, LANES), F32), pltpu.VMEM((rows, LANES), F32)],
        compiler_params=_params("parallel", "arbitrary"),
        name="fox_attn",
    )(q, kd, vd, fcum, kp, vp, fcum)


def _ssd_kernel(xbc_ref, z_ref, sm_ref, smt_ref, cinit_ref, h0_ref, cw_ref, cb_ref, alane_ref, acol_ref,
                dskip_ref, g_ref, y_ref, hfin_ref, xp_ref, h_ref, *, L):
    @pl.when(pl.program_id(1) == 0)
    def _():
        h_ref[...] = h0_ref[0]
        xp_ref[0:CONV_PAD, :] = cinit_ref[0]

    xp_ref[CONV_PAD:CONV_PAD + L, :] = xbc_ref[0]
    first = CONV_PAD - (SSD_CONV - 1)
    conv = cb_ref[...]
    for i in range(SSD_CONV):
        conv = conv + cw_ref[i:i + 1, :] * xp_ref[first + i:first + i + L, :]
    xp_ref[0:CONV_PAD, :] = xp_ref[L:L + CONV_PAD, :]
    xc = _silu(conv)
    xs = xc[:, :SSD_INNER]

    dt_col = sm_ref[0]
    a_col = dt_col * alane_ref[...]
    ri = lax.broadcasted_iota(jnp.int32, (L, L), 0)
    ci = lax.broadcasted_iota(jnp.int32, (L, L), 1)
    causal = ci <= ri
    tri_lower = jnp.where(causal, 1.0, 0.0).astype(BF16)
    tri_upper = jnp.where(ri <= ci, 1.0, 0.0).astype(BF16)
    acum_col = sum(_dot(tri_lower, part) for part in _split3(a_col))
    a_row = smt_ref[0] * acol_ref[...]
    acum_row = _dot(jnp.concatenate(_split3(a_row), axis=1),
                    jnp.concatenate([tri_upper] * 3, axis=0))
    a_last = acum_col[L - 1:L, :]

    ys = []
    heads_per_group = SSD_HEADS // SSD_GROUPS
    for grp in range(SSD_GROUPS):
        b0 = SSD_INNER + grp * SSD_STATE
        c0 = SSD_INNER + SSD_GROUPS * SSD_STATE + grp * SSD_STATE
        bg = xc[:, b0:b0 + SSD_STATE].astype(BF16)
        cg = xc[:, c0:c0 + SSD_STATE].astype(BF16)
        cb = _dot_nt(cg, bg)
        for hh in range(heads_per_group):
            h = grp * heads_per_group + hh
            col = SMALL_DT + h
            ac = acum_col[:, col:col + 1]
            ar = acum_row[col:col + 1, :]
            decay = jnp.exp(jnp.where(causal, ac - ar, NEG))
            scores = (cb * decay).astype(BF16)
            xdt = xs[:, h * SSD_HEAD_DIM:(h + 1) * SSD_HEAD_DIM] * dt_col[:, col:col + 1]
            state = h_ref[h]
            y_h = _dot(scores, xdt.astype(BF16)) + _dot_nt(cg, state.astype(BF16)) * jnp.exp(ac)
            al = a_last[:, col:col + 1]
            xw = (xdt * jnp.exp(al - ac)).astype(BF16)
            h_ref[h] = jnp.exp(al) * state + lax.dot_general(xw, bg, TN_DIMS, preferred_element_type=F32)
            ys.append(y_h)
    y = jnp.concatenate(ys, axis=1) + dskip_ref[...] * xs
    y_ref[0] = _rms(y * _silu(z_ref[0]), g_ref[...])
    hfin_ref[0] = h_ref[...]


def _ssd(xbc, z, small, small_t, conv_init, h0, p, layer, L):
    b, n, _ = xbc.shape
    weights = [p["conv_w"], p["conv_b"], p["a_lane"], p["a_col"], p["dskip_vec"], p["ssd_norm_g"]]
    seq = lambda w: pl.BlockSpec((1, L, w), lambda i, j: (i, j, 0))
    per_b = lambda *s: pl.BlockSpec((1,) + s, lambda i, j: (i,) + (0,) * len(s))
    return pl.pallas_call(
        functools.partial(_ssd_kernel, L=L),
        grid=(b, n // L),
        in_specs=[seq(CONV_DIM), seq(SSD_INNER), seq(LANES),
                  pl.BlockSpec((1, small_t.shape[1], L), lambda i, j: (i, 0, j)),
                  per_b(CONV_PAD, CONV_DIM), per_b(SSD_HEADS, SSD_HEAD_DIM, SSD_STATE)]
        + [_wspec(w, layer) for w in weights],
        out_specs=[seq(SSD_INNER), per_b(SSD_HEADS, SSD_HEAD_DIM, SSD_STATE)],
        out_shape=[jax.ShapeDtypeStruct((b, n, SSD_INNER), F32),
                   jax.ShapeDtypeStruct((b, SSD_HEADS, SSD_HEAD_DIM, SSD_STATE), F32)],
        scratch_shapes=[pltpu.VMEM((L + CONV_PAD, CONV_DIM), F32),
                        pltpu.VMEM((SSD_HEADS, SSD_HEAD_DIM, SSD_STATE), F32)],
        compiler_params=_params("parallel", "arbitrary"),
        name="ssd",
    )(xbc, z, small, small_t, conv_init, h0, *weights)


def _post_mix_kernel(h_ref, sb_ref, fx_ref, ssd_ref, wout_ref, gx_ref, wq_ref, mk_ref, mv_ref, wo_ref, o_ref,
                     h1_ref, q_ref, att_ref, *, rows_per_batch):
    j = pl.program_id(1)
    d = h_ref.shape[1]
    xd = d // X_HEADS

    @pl.when(j == 0)
    def _():
        mix = (_dot(sb_ref[...].astype(BF16), wout_ref[0:ATT_WIDTH, :])
               + _dot(fx_ref[...].astype(BF16), wout_ref[ATT_WIDTH:2 * ATT_WIDTH, :])
               + _dot(ssd_ref[...].astype(BF16), wout_ref[2 * ATT_WIDTH:, :]))
        h1 = h_ref[...] + mix
        h1_ref[...] = h1
        u = _rms(h1, gx_ref[...]).astype(BF16)
        q_ref[...] = (_dot(u, wq_ref[...]) * (LOG2E * xd ** -0.5)).astype(BF16)

    rows = pl.ds(pl.multiple_of(j * rows_per_batch, rows_per_batch), rows_per_batch)
    for hd in range(X_HEADS):
        cols = slice(hd * xd, (hd + 1) * xd)
        s = _dot_nt(q_ref[rows, cols], mk_ref[:, cols].astype(BF16))
        p = jnp.exp2(s - jnp.max(s, axis=1, keepdims=True))
        o = _dot(p.astype(BF16), mv_ref[:, cols].astype(BF16))
        att_ref[rows, cols] = (o * (1.0 / jnp.sum(p, axis=1, keepdims=True))).astype(BF16)

    @pl.when(j == pl.num_programs(1) - 1)
    def _():
        o_ref[...] = h1_ref[...] + _dot(att_ref[...], wo_ref[...])


def _post_mix(h, sb_o, fox_o, ssd_o, p, layer, mk, mv, n, tm):
    m, d = h.shape
    rows_per_batch = min(tm, n)
    row = lambda w: pl.BlockSpec((tm, w), lambda i, j: (i, 0))
    mem_spec = pl.BlockSpec((None, None) + mk.shape[2:], lambda i, j: (layer, (i * tm) // n + j, 0, 0))
    w_out, gx, wq, wo = p["w_out"], p["norm_x_g"], p["wq_x"], p["wo_x"]
    return pl.pallas_call(
        functools.partial(_post_mix_kernel, rows_per_batch=rows_per_batch),
        grid=(m // tm, tm // rows_per_batch),
        in_specs=[row(d), row(ATT_WIDTH), row(ATT_WIDTH), row(SSD_INNER), _wspec(w_out, layer),
                  _wspec(gx, layer), _wspec(wq, layer), mem_spec, mem_spec, _wspec(wo, layer)],
        out_specs=row(d),
        out_shape=jax.ShapeDtypeStruct(h.shape, F32),
        scratch_shapes=[pltpu.VMEM((tm, d), F32), pltpu.VMEM((tm, d), BF16), pltpu.VMEM((tm, d), BF16)],
        compiler_params=_params("parallel", "arbitrary"),
        name="post_mix",
    )(h, sb_o, fox_o, ssd_o, w_out, gx, wq, mk, mv, wo)


def _ffn_kernel(h_ref, g_ref, wg_ref, wu_ref, wd_ref, gf_ref, o_ref, *, final_norm):
    h = h_ref[...]
    u = _rms(h, g_ref[...]).astype(BF16)
    hidden = (_silu(_dot(u, wg_ref[...])) * _dot(u, wu_ref[...])).astype(BF16)
    out = h + _dot(hidden, wd_ref[...])
    if final_norm:
        out = _rms(out, gf_ref[...])
    o_ref[...] = out


def _ffn(h, p, layer, g_final, final_norm, tm):
    m, d = h.shape
    row = pl.BlockSpec((tm, d), lambda i: (i, 0))
    weights = [p["norm_ffn_g"], p["w_gate"], p["w_up"], p["w_down"]]
    return pl.pallas_call(
        functools.partial(_ffn_kernel, final_norm=final_norm),
        grid=(m // tm,),
        in_specs=[row] + [_wspec(w, layer) for w in weights] + [_wspec(g_final)],
        out_specs=row,
        out_shape=jax.ShapeDtypeStruct(h.shape, F32),
        compiler_params=_params("parallel"),
        name="ffn",
    )(h, *weights, g_final)


def _mem_kv_kernel(x_ref, g_ref, wk_ref, wv_ref, k_ref, v_ref, kb_ref, vb_ref):
    u = _rms(x_ref[...], g_ref[...]).astype(BF16)
    xd = k_ref.shape[2]
    for w_ref, o_ref, b_ref in ((wk_ref, k_ref, kb_ref), (wv_ref, v_ref, vb_ref)):
        y = _dot(u, w_ref[...])
        b_ref[...] = y.astype(BF16)
        for hd in range(X_HEADS):
            o_ref[:, hd, :] = y[:, hd * xd:(hd + 1) * xd]


def _mem_kv(mem, p):
    b, t, d = mem.shape
    weights = [p["mem_norm_g"], p["wk_x"], p["wv_x"]]
    layers = weights[0].shape[0]
    heads_block = (t, X_HEADS, d // X_HEADS)
    heads_spec = pl.BlockSpec((None, None) + heads_block, lambda l, i: (l, i, 0, 0, 0))
    rows_spec = pl.BlockSpec((None, None, t, d), lambda l, i: (l, i, 0, 0))
    w_spec = lambda w: pl.BlockSpec((None,) + w.shape[1:], lambda l, i: (l, 0, 0))
    return pl.pallas_call(
        _mem_kv_kernel,
        grid=(layers, b),
        in_specs=[pl.BlockSpec((None, t, d), lambda l, i: (i, 0, 0))] + [w_spec(w) for w in weights],
        out_specs=[heads_spec, heads_spec, rows_spec, rows_spec],
        out_shape=[jax.ShapeDtypeStruct((layers, b) + heads_block, F32)] * 2
        + [jax.ShapeDtypeStruct((layers, b, t, d), BF16)] * 2,
        compiler_params=_params("parallel", "parallel"),
        name="mem_kv",
    )(mem, *weights)


def _seq_minor(a):
    layers, b, n = a.shape[:3]
    return jnp.transpose(a, (0, 1, 3, 4, 2)).reshape(layers, b, -1, n)


def _layer(h, p, layer, past, mem_k, mem_v, final_g, final_norm, kv_stacked=None):
    b, n, d = h.shape
    m = b * n
    prompt = past is None
    tm = min(m, 512)
    outs = _in_proj(h.reshape(m, d), p, layer, tm, b if prompt else None, kv_stacked)
    sq, fq, z, xbc, small = (a.reshape(b, n, a.shape[-1]) for a in outs[:5])
    small_t = outs[5]
    kv_t = outs[6:]
    if not prompt:
        per_batch = lambda a: jnp.transpose(a.reshape(a.shape[0], b, n), (1, 0, 2))
        small_t = per_batch(small_t)
        kv_t = [per_batch(a) for a in kv_t]
    skt, svt, fkt, fvt, skb, svb, fkb, fvb = kv_t
    logf_t = small_t[:, SMALL_LOGF:SMALL_LOGF + ATT_HEADS, :]

    if prompt:
        fcum = _seq_cumsum(small_t[:, :CONV_PAD, :])
        sb_o = _sb_attn(sq, skb, svb, skb, svb)
        fox_o = _fox_attn(fq, fkb, fvb, fkb, fvb, fcum)
        conv_init = jnp.zeros((b, CONV_PAD, CONV_DIM), F32)
        h0 = jnp.zeros((b, SSD_HEADS, SSD_HEAD_DIM, SSD_STATE), F32)
        conv_hist = None
        ssd_len = ATT_BLOCK
    else:
        pad_keys = lambda a: jnp.pad(a, ((0, 0), (0, 0), (0, ATT_BLOCK - n)))
        sb_o = _sb_attn(sq, pad_keys(skb), pad_keys(svb), past["sb_k"], past["sb_v"], layer)
        logf_all = jnp.concatenate([jnp.transpose(past["fox_logf"], (0, 2, 1)), logf_t], axis=2)
        logf_all = jnp.pad(logf_all, ((0, 0), (0, CONV_PAD - ATT_HEADS), (0, ATT_BLOCK - n)))
        fcum = _seq_cumsum(logf_all)
        fox_o = _fox_attn(fq, pad_keys(fkb), pad_keys(fvb), past["fox_k"], past["fox_v"], fcum, layer)
        conv_hist = past["conv"]
        conv_init = jnp.pad(conv_hist, ((0, 0), (CONV_PAD - (SSD_CONV - 1), 0), (0, 0)))
        h0 = past["ssm"]
        ssd_len = n

    ssd_o, h_final = _ssd(xbc, z, small, small_t, conv_init, h0, p, layer, ssd_len)
    if conv_hist is None:
        conv_state = xbc[:, n - (SSD_CONV - 1):, :]
    else:
        conv_state = jnp.concatenate([conv_hist, xbc], axis=1)[:, -(SSD_CONV - 1):, :]

    flat = lambda a: a.reshape(m, a.shape[-1])
    h1 = _post_mix(flat(h), flat(sb_o), flat(fox_o), flat(ssd_o), p, layer, mem_k, mem_v, n, tm)
    h2 = _ffn(h1, p, layer, final_g, final_norm, tm)
    return h2.reshape(b, n, d), (skt, svt, fkt, fvt, logf_t, h_final, conv_state)


def _stacked_params(norm_mix_g, w_in, fox_b_f, conv_w, conv_b, dt_bias, a_log, d_skip, ssd_norm_g, w_out,
                    norm_x_g, mem_norm_g, wq_x, wk_x, wv_x, wo_x, norm_ffn_g, w_gate, w_up, w_down):
    aw = ATT_WIDTH
    qkv = 6 * aw
    gate_end = qkv + ATT_HEADS
    z_end = gate_end + SSD_INNER
    xbc_end = z_end + CONV_DIM
    w = jnp.transpose(w_in, (0, 2, 1))
    layers, _, d = w.shape
    row = lambda a: a.reshape(layers, 1, -1).astype(F32)
    small_pad = LANES - (ATT_HEADS + SSD_HEADS)
    zeros = lambda n: jnp.zeros((layers, n), F32)
    a_small = jnp.concatenate([zeros(ATT_HEADS), -jnp.exp(a_log.astype(F32)), zeros(small_pad)], axis=1)
    b_small = jnp.concatenate([fox_b_f.astype(F32), dt_bias.astype(F32), zeros(small_pad)], axis=1)
    col = lambda a: a[:, :2 * CONV_PAD].reshape(layers, 2 * CONV_PAD, 1)
    bf = lambda a: a.astype(BF16)
    return dict(
        norm_mix_g=row(norm_mix_g),
        wq_t=bf(jnp.concatenate([w[:, 0:aw], w[:, 3 * aw:4 * aw]], axis=1)),
        wkv_t=bf(jnp.concatenate([w[:, aw:3 * aw], w[:, 4 * aw:6 * aw]], axis=1)),
        wzx_t=bf(w[:, gate_end:xbc_end]),
        ws_t=bf(jnp.concatenate([w[:, qkv:gate_end], w[:, xbc_end:], jnp.zeros((layers, small_pad, d), w.dtype)],
                                axis=1)),
        b_small_row=row(b_small), b_small_col=col(b_small),
        conv_w=conv_w.astype(F32), conv_b=row(conv_b),
        a_lane=row(a_small), a_col=col(a_small),
        dskip_vec=row(jnp.repeat(d_skip, SSD_HEAD_DIM, axis=1)), ssd_norm_g=row(ssd_norm_g),
        w_out=bf(w_out), norm_x_g=row(norm_x_g), mem_norm_g=row(mem_norm_g),
        wq_x=bf(wq_x), wk_x=bf(wk_x), wv_x=bf(wv_x), wo_x=bf(wo_x), norm_ffn_g=row(norm_ffn_g),
        w_gate=bf(w_gate), w_up=bf(w_up), w_down=bf(w_down))


def kernel(x_prompt, x_sample, cache_sb_k, cache_sb_v, cache_fox_k, cache_fox_v, cache_fox_logf, state_ssm, state_conv, cache_mem_k, cache_mem_v, mem_prompt, norm_mix_g, w_in, fox_b_f, conv_w, conv_b, dt_bias, a_log, d_skip, ssd_norm_g, w_out, norm_x_g, mem_norm_g, wq_x, wk_x, wv_x, wo_x, norm_ffn_g, w_gate, w_up, w_down, final_norm_g):
    depth = w_in.shape[0]
    bp, mem_len, d = mem_prompt.shape
    bs = x_sample.shape[0]
    final_g = final_norm_g.reshape(1, d).astype(F32)
    p = _stacked_params(norm_mix_g, w_in, fox_b_f, conv_w, conv_b, dt_bias, a_log, d_skip, ssd_norm_g, w_out,
                        norm_x_g, mem_norm_g, wq_x, wk_x, wv_x, wo_x, norm_ffn_g, w_gate, w_up, w_down)
    caches_t = [_seq_minor(c) for c in (cache_sb_k, cache_sb_v, cache_fox_k, cache_fox_v)]
    hp, hs = x_prompt, x_sample
    p_mk, p_mv, p_mk_rows, p_mv_rows = _mem_kv(mem_prompt, p)
    s_mk_rows = cache_mem_k.reshape(depth, bs, mem_len, d)
    s_mv_rows = cache_mem_v.reshape(depth, bs, mem_len, d)
    p_st, s_st = [], []
    p_kv = None
    for l in range(depth):
        last = l == depth - 1
        hp, st = _layer(hp, p, l, None, p_mk_rows, p_mv_rows, final_g, last, p_kv)
        p_kv = st[:4]
        p_st.append(st)
        past = dict(sb_k=caches_t[0], sb_v=caches_t[1], fox_k=caches_t[2], fox_v=caches_t[3],
                    fox_logf=cache_fox_logf[l], ssm=state_ssm[l], conv=state_conv[l])
        hs, st = _layer(hs, p, l, past, s_mk_rows, s_mv_rows, final_g, last)
        s_st.append(st)
    stk = lambda sts, i: jnp.stack([s[i] for s in sts], axis=0)
    kv = lambda a: jnp.transpose(a.reshape(depth, a.shape[1], ATT_HEADS, HEAD_DIM, -1), (0, 1, 4, 2, 3))
    logf = lambda sts: jnp.transpose(stk(sts, 4), (0, 1, 3, 2))
    return (hp, hs,
            kv(p_kv[0]), kv(p_kv[1]), kv(p_kv[2]), kv(p_kv[3]), logf(p_st), stk(p_st, 5), stk(p_st, 6),
            p_mk, p_mv,
            kv(stk(s_st, 0)), kv(stk(s_st, 1)), kv(stk(s_st, 2)), kv(stk(s_st, 3)), logf(s_st),
            stk(s_st, 5), stk(s_st, 6))
```

```python
import functools

import jax
import jax.numpy as jnp
from jax import lax
from jax.experimental import pallas as pl
from jax.experimental.pallas import tpu as pltpu

F32 = jnp.float32
BF16 = jnp.bfloat16
EPS = 1e-6

HEAD_DIM = 64
ATT_HEADS = 4
ATT_WIDTH = ATT_HEADS * HEAD_DIM
SSD_HEADS = 8
SSD_HEAD_DIM = 64
SSD_INNER = SSD_HEADS * SSD_HEAD_DIM
SSD_GROUPS = 2
SSD_STATE = 128
SSD_CONV = 4
CONV_DIM = SSD_INNER + 2 * SSD_GROUPS * SSD_STATE
X_HEADS = 4
LANES = 128
SMALL_LOGF = 0
SMALL_DT = ATT_HEADS
CONV_PAD = 8
ATT_BLOCK = 256
SAMPLE_PAST_BLOCK = 1024
NEG = -1e30
LOG2E = 1.4426950408889634
SB_DEAD_LOG2 = 150.0
SB_RECENT_BLOCKS = 2
VMEM_LIMIT = 56 * 1024 * 1024

NT_DIMS = (((1,), (1,)), ((), ()))
TN_DIMS = (((0,), (0,)), ((), ()))


def _params(*sem):
    return pltpu.CompilerParams(dimension_semantics=sem, vmem_limit_bytes=VMEM_LIMIT)


def _rms(x, g):
    ms = jnp.mean(x * x, axis=-1, keepdims=True)
    return x * lax.rsqrt(ms + EPS) * g


def _softplus(x):
    t = jnp.exp2(jnp.abs(x) * (-LOG2E))
    return jnp.maximum(x, 0.0) + jnp.log(1.0 + t)


def _softplus2(x):
    return jnp.maximum(x, 0.0) + jnp.log2(1.0 + jnp.exp2(-jnp.abs(x)))


def _silu(x):
    half = 0.5 * x
    return half + half * jnp.tanh(half)


def _dot(a, b):
    return jnp.dot(a, b, preferred_element_type=F32)


def _dot_nt(a, b):
    return lax.dot_general(a, b, NT_DIMS, preferred_element_type=F32)


def _split3(x):
    hi = x.astype(BF16)
    r1 = x - hi.astype(F32)
    mid = r1.astype(BF16)
    lo = (r1 - mid.astype(F32)).astype(BF16)
    return hi, mid, lo


def _wspec(a, layer=None):
    once = pl.Buffered(1)
    if layer is None:
        return pl.BlockSpec(a.shape, lambda *_: (0,) * a.ndim, pipeline_mode=once)
    return pl.BlockSpec((None,) + a.shape[1:], lambda *_: (layer,) + (0,) * (a.ndim - 1), pipeline_mode=once)


def _in_proj_kernel(*refs, n_prev):
    x_ref, g_ref, wq_ref, wkv_ref, wzx_ref, ws_ref, bs_row_ref, bs_col_ref = refs[:8]
    prev_refs = refs[8:8 + n_prev]
    (sq_ref, fq_ref, z_ref, xbc_ref, small_ref, small_t_ref,
     skt_ref, svt_ref, fkt_ref, fvt_ref, skb_ref, svb_ref, fkb_ref, fvb_ref) = refs[8 + n_prev:]
    u = _rms(x_ref[...], g_ref[...]).astype(BF16)
    q = _dot_nt(u, wq_ref[...]) * (LOG2E * HEAD_DIM ** -0.5)
    sq_ref[...] = q[:, :ATT_WIDTH].astype(BF16)
    fq_ref[...] = q[:, ATT_WIDTH:].astype(BF16)
    zx = _dot_nt(u, wzx_ref[...])
    z_ref[...] = zx[:, :SSD_INNER]
    xbc_ref[...] = zx[:, SSD_INNER:]
    kvt = _dot_nt(wkv_ref[...], u)
    for i, (f_ref, b_ref) in enumerate(((skt_ref, skb_ref), (svt_ref, svb_ref),
                                        (fkt_ref, fkb_ref), (fvt_ref, fvb_ref))):
        part = kvt[i * ATT_WIDTH:(i + 1) * ATT_WIDTH, :]
        if f_ref.ndim == 3:
            if n_prev:
                f_ref[0:f_ref.shape[0] - 1] = prev_refs[i][...]
            f_ref[f_ref.shape[0] - 1] = part
        else:
            f_ref[...] = part
        b_ref[...] = part.astype(BF16).reshape(b_ref.shape)

    def gates(s, index):
        is_gate = index < SMALL_DT
        sp = _softplus(jnp.where(is_gate, -s, s))
        return jnp.where(is_gate, -sp, sp)

    s = _dot_nt(u, ws_ref[...]) + bs_row_ref[...]
    small_ref[...] = gates(s, lax.broadcasted_iota(jnp.int32, s.shape, 1))
    st = _dot_nt(ws_ref[0:2 * CONV_PAD, :], u) + bs_col_ref[...]
    small_t_ref[...] = gates(st, lax.broadcasted_iota(jnp.int32, st.shape, 0)).reshape(small_t_ref.shape)


def _in_proj(x, p, layer, tm, batch, stacked=None):
    m, d = x.shape
    row = lambda n: pl.BlockSpec((tm, n), lambda i: (i, 0))
    weights = [p["norm_mix_g"], p["wq_t"], p["wkv_t"], p["wzx_t"], p["ws_t"], p["b_small_row"], p["b_small_col"]]
    rows = [(ATT_WIDTH, BF16), (ATT_WIDTH, BF16), (SSD_INNER, F32), (CONV_DIM, F32), (LANES, F32)]
    stacked = list(stacked or [])
    if batch is None:
        t_shape = kv_shape = lambda r: (r, m)
        t_spec = kv_spec = lambda r: pl.BlockSpec((r, tm), lambda i: (0, i))
        prev_specs = []
    else:
        nb = m // batch // tm
        t_shape = lambda r: (batch, r, m // batch)
        t_spec = lambda r: pl.BlockSpec((1, r, tm), lambda i: (i // nb, 0, i % nb))
        stack_spec = lambda n, r: pl.BlockSpec((n, None, r, tm), lambda i: (0, i // nb, 0, i % nb))
        kv_shape = lambda r: (layer + 1, batch, r, m // batch)
        kv_spec = lambda r: stack_spec(layer + 1, r)
        prev_specs = [stack_spec(layer, ATT_WIDTH)] * len(stacked)
    out_specs = ([row(n) for n, _ in rows] + [t_spec(2 * CONV_PAD)] + [kv_spec(ATT_WIDTH)] * 4
                 + [t_spec(ATT_WIDTH)] * 4)
    out_shape = ([jax.ShapeDtypeStruct((m, n), dt) for n, dt in rows]
                 + [jax.ShapeDtypeStruct(t_shape(2 * CONV_PAD), F32)]
                 + [jax.ShapeDtypeStruct(kv_shape(ATT_WIDTH), F32)] * 4
                 + [jax.ShapeDtypeStruct(t_shape(ATT_WIDTH), BF16)] * 4)
    return pl.pallas_call(
        functools.partial(_in_proj_kernel, n_prev=len(stacked)),
        grid=(m // tm,),
        in_specs=[row(d)] + [_wspec(w, layer) for w in weights] + prev_specs,
        out_specs=out_specs,
        out_shape=out_shape,
        compiler_params=_params("parallel"),
        name="in_proj",
    )(x, *weights, *stacked)


def _cumsum_kernel(x_ref, o_ref, *, blk):
    n = x_ref.shape[2]
    r = lax.broadcasted_iota(jnp.int32, (blk, blk), 0)
    c = lax.broadcasted_iota(jnp.int32, (blk, blk), 1)
    tri = jnp.where(r <= c, 1.0, 0.0).astype(BF16)
    tri3 = jnp.concatenate([tri, tri, tri], axis=0)
    carry = jnp.zeros((x_ref.shape[1], 1), F32)
    for j in range(n // blk):
        xb = x_ref[0, :, j * blk:(j + 1) * blk]
        out = _dot(jnp.concatenate(_split3(xb), axis=1), tri3) + carry
        o_ref[0, :, j * blk:(j + 1) * blk] = out
        carry = out[:, blk - 1:blk]


def _seq_cumsum(x):
    b, r, n = x.shape
    spec = pl.BlockSpec((1, b * r, n), lambda i: (0, 0, 0))
    return pl.pallas_call(
        functools.partial(_cumsum_kernel, blk=ATT_BLOCK),
        grid=(1,),
        in_specs=[spec],
        out_specs=spec,
        out_shape=jax.ShapeDtypeStruct((1, b * r, n), F32),
        compiler_params=_params("arbitrary"),
        name="seq_cumsum",
    )(x.reshape(1, b * r, n)).reshape(b, r, n)


def _stack_heads(q, qbd_ref, tq):
    lane_head = lax.broadcasted_iota(jnp.int32, q.shape, 1) // HEAD_DIM
    for h in range(ATT_HEADS):
        qbd_ref[h * tq:(h + 1) * tq, :] = jnp.where(lane_head == h, q, jnp.zeros_like(q))


def _unstack_heads(acc, tq):
    lane_head = lax.broadcasted_iota(jnp.int32, (tq, ATT_WIDTH), 1) // HEAD_DIM
    out = jnp.zeros((tq, ATT_WIDTH), F32)
    for h in range(ATT_HEADS):
        out = jnp.where(lane_head == h, acc[h * tq:(h + 1) * tq, :], out)
    return out


def _local_positions(rows, tq, tk):
    t_loc = lax.broadcasted_iota(jnp.int32, (rows, tk), 0) & (tq - 1)
    s_loc = lax.broadcasted_iota(jnp.int32, (rows, tk), 1)
    return t_loc, s_loc


def _lane_tile(x, width):
    return jnp.concatenate([x] * (width // LANES), axis=1)


def _past_loop(run_at, n_blocks, tp, keep_going=None):
    def run(step):
        run_at(pl.multiple_of((n_blocks - 1 - step) * tp, tp))

    if keep_going is None:
        def body(step, carry):
            run(step)
            return carry

        lax.fori_loop(0, n_blocks, body, 0)
    else:
        def body(state):
            run(state[0])
            return state[0] + 1, keep_going()

        lax.while_loop(lambda state: (state[0] < n_blocks) & (state[1] > 0), body,
                       (jnp.int32(0), keep_going()))


def _attn_specs(tq, td, prompt, layer):
    q_spec = pl.BlockSpec((None, tq, ATT_WIDTH), lambda i, j: (i, j, 0))
    if prompt:
        edge = lambda r: pl.BlockSpec((None, r, td), lambda i, j: (i, 0, j))
        past = lambda a: pl.BlockSpec((None,) + a.shape[1:], lambda i, j: (i, 0, 0))
    else:
        edge = lambda r: pl.BlockSpec((None, r, td), lambda i, j: (i, 0, 0))
        past = lambda a: pl.BlockSpec((None, None) + a.shape[2:], lambda i, j: (layer, i, 0, 0))
    return q_spec, edge, past


def _sb_kernel(*refs, tq, tp, n_past, n_recent, layer):
    n_past_inputs = 2 if n_past is None else 4
    q_ref, kd_ref, vd_ref = refs[:3]
    o_ref = refs[3 + n_past_inputs]
    qbd_ref, acc_ref, carry_ref = refs[4 + n_past_inputs:7 + n_past_inputs]
    past_refs = refs[3:3 + n_past_inputs] + refs[7 + n_past_inputs:]
    rows = ATT_HEADS * tq
    _stack_heads(q_ref[...], qbd_ref, tq)
    acc_ref[...] = jnp.zeros_like(acc_ref)
    carry_ref[...] = jnp.zeros_like(carry_ref)
    r = lax.broadcasted_iota(jnp.int32, (2 * LANES, 2 * LANES), 0) & (LANES - 1)
    c = lax.broadcasted_iota(jnp.int32, (2 * LANES, 2 * LANES), 1)
    wcum = jnp.where((c >= LANES) | (r > c), -1.0, 0.0).astype(BF16)

    def block(kt, vt, diagonal=False):
        kt = kt.astype(BF16)
        vt = vt.astype(BF16)
        tk = kt.shape[1]
        s = _dot(qbd_ref[...], kt)
        sp = _softplus2(s)
        log_beta = s - sp
        if diagonal:
            t_loc, s_loc = _local_positions(rows, tq, tk)
            valid = s_loc < t_loc
            sp = jnp.where(valid, sp, 0.0)
        hi = sp.astype(BF16)
        lo = (sp - hi.astype(F32)).astype(BF16)
        carry = carry_ref[...]
        after = [None] * (tk // LANES)
        for g in reversed(range(tk // LANES)):
            cols = slice(g * LANES, (g + 1) * LANES)
            a = _dot(jnp.concatenate([hi[:, cols], lo[:, cols]], axis=1), wcum)
            after[g] = a[:, :LANES] + carry
            carry = carry + a[:, LANES:]
        carry_ref[...] = carry
        w = jnp.exp2(log_beta + jnp.concatenate(after, axis=1))
        if diagonal:
            w = jnp.where(valid, w, 0.0)
        acc_ref[...] += _dot_nt(w.astype(BF16), vt)

    def any_weight_left():
        return (jnp.max(carry_ref[...]) > -SB_DEAD_LOG2).astype(jnp.int32)

    block(kd_ref[...], vd_ref[...], diagonal=True)
    if n_past is None:
        kp_ref, vp_ref = past_refs

        def run_at(start):
            block(kp_ref[:, pl.ds(start, tp)], vp_ref[:, pl.ds(start, tp)])

        n_blocks = pl.program_id(1)
    else:
        kr_ref, vr_ref, k_all, v_all, kbuf_ref, vbuf_ref, sem = past_refs
        recent_start = (n_past - n_recent) * tp
        batch = pl.program_id(0)

        def run_at(start):
            @pl.when(start >= recent_start)
            def _():
                off = pl.multiple_of(start - recent_start, tp)
                block(kr_ref[:, pl.ds(off, tp)], vr_ref[:, pl.ds(off, tp)])

            @pl.when(start < recent_start)
            def _():
                copies = [pltpu.make_async_copy(src.at[layer, batch, :, pl.ds(start, tp)], dst, sem.at[i])
                          for i, (src, dst) in enumerate(((k_all, kbuf_ref), (v_all, vbuf_ref)))]
                for c in copies:
                    c.start()
                for c in copies:
                    c.wait()
                block(kbuf_ref[...], vbuf_ref[...])

        n_blocks = n_past
    _past_loop(run_at, n_blocks, tp, keep_going=any_weight_left)
    o_ref[...] = _unstack_heads(acc_ref[...], tq)


def _sb_attn(q, kd, vd, kp, vp, layer=None):
    b, n, _ = q.shape
    prompt = layer is None
    td = ATT_BLOCK
    tq = ATT_BLOCK if prompt else n
    rows = ATT_HEADS * tq
    q_spec, edge, past = _attn_specs(tq, td, prompt, layer)
    scratch = [pltpu.VMEM((rows, ATT_WIDTH), BF16), pltpu.VMEM((rows, ATT_WIDTH), F32),
               pltpu.VMEM((rows, LANES), F32)]
    if prompt:
        n_past = None
        past_specs = [past(kp), past(vp)]
        past_args = (kp, vp)
    else:
        n_past = kp.shape[-1] // td
        n_recent = min(SB_RECENT_BLOCKS, n_past)
        last_window = n_past // n_recent - 1
        recent = pl.BlockSpec((None, None, ATT_WIDTH, n_recent * td), lambda i, j: (layer, i, 0, last_window))
        whole = pl.BlockSpec(memory_space=pl.ANY)
        past_specs = [recent, recent, whole, whole]
        past_args = (kp, vp, kp, vp)
        scratch += [pltpu.VMEM((ATT_WIDTH, td), kp.dtype), pltpu.VMEM((ATT_WIDTH, td), vp.dtype),
                    pltpu.SemaphoreType.DMA((2,))]
    return pl.pallas_call(
        functools.partial(_sb_kernel, tq=tq, tp=td, n_past=n_past,
                          n_recent=None if prompt else n_recent, layer=layer),
        grid=(b, n // tq),
        in_specs=[q_spec, edge(ATT_WIDTH), edge(ATT_WIDTH)] + past_specs,
        out_specs=q_spec,
        out_shape=jax.ShapeDtypeStruct((b, n, ATT_WIDTH), F32),
        scratch_shapes=scratch,
        compiler_params=_params("parallel", "arbitrary"),
        name="sb_attn",
    )(q, kd, vd, *past_args)


def _fox_kernel(q_ref, kd_ref, vd_ref, fd_ref, kp_ref, vp_ref, fp_ref, o_ref,
                qbd_ref, acc_ref, m_ref, l_ref, *, tq, td, tp, n_past):
    rows = ATT_HEADS * tq
    _stack_heads(q_ref[...], qbd_ref, tq)
    acc_ref[...] = jnp.zeros_like(acc_ref)
    m_ref[...] = jnp.full_like(m_ref, NEG)
    l_ref[...] = jnp.zeros_like(l_ref)

    def block(kt, vt, fk, diagonal=False):
        kt = kt.astype(BF16)
        vt = vt.astype(BF16)
        fk = fk * LOG2E
        tk = kt.shape[1]
        s = _dot(qbd_ref[...], kt)
        sh = jnp.concatenate([s[h * tq:(h + 1) * tq] - fk[h:h + 1, :] for h in range(ATT_HEADS)], axis=0)
        if diagonal:
            t_loc, s_loc = _local_positions(rows, tq, tk)
            sh = jnp.where(s_loc <= t_loc, sh, NEG)
        m_old = m_ref[...]
        m_new = jnp.maximum(m_old, jnp.max(sh, axis=1, keepdims=True))
        alpha = jnp.exp2(m_old - m_new)
        p = jnp.exp2(sh - _lane_tile(m_new, tk))
        l_ref[...] = alpha * l_ref[...] + jnp.sum(p, axis=1, keepdims=True)
        m_ref[...] = m_new
        acc_ref[...] = acc_ref[...] * _lane_tile(alpha, ATT_WIDTH) + _dot_nt(p.astype(BF16), vt)

    block(kd_ref[...], vd_ref[...], fd_ref[...], diagonal=True)
    def run(start, size):
        block(kp_ref[:, pl.ds(start, size)], vp_ref[:, pl.ds(start, size)], fp_ref[:, pl.ds(start, size)])

    if n_past is None:
        n_edge = pl.program_id(1)

        @pl.when(n_edge % 2 == 1)
        def _():
            run(pl.multiple_of((n_edge - 1) * td, td), td)

        _past_loop(functools.partial(run, size=tp), n_edge // 2, tp)
    else:
        _past_loop(functools.partial(run, size=tp), n_past, tp)
    o_ref[...] = _unstack_heads(acc_ref[...] * _lane_tile(1.0 / l_ref[...], ATT_WIDTH), tq)


def _fox_attn(q, kd, vd, kp, vp, fcum, layer=None):
    b, n, _ = q.shape
    prompt = layer is None
    td = ATT_BLOCK
    tq = ATT_BLOCK if prompt else n
    tp = 2 * ATT_BLOCK if prompt else SAMPLE_PAST_BLOCK
    rows = ATT_HEADS * tq
    q_spec, edge, past = _attn_specs(tq, td, prompt, layer)
    n_past = None if prompt else kp.shape[-1] // tp
    fr = fcum.shape[1]
    if prompt:
        fd_spec = edge(fr)
    else:
        last = kp.shape[-1] // td
        fd_spec = pl.BlockSpec((None, fr, td), lambda i, j: (i, 0, last))
    fp_spec = pl.BlockSpec((None,) + fcum.shape[1:], lambda i, j: (i, 0, 0))
    return pl.pallas_call(
        functools.partial(_fox_kernel, tq=tq, td=td, tp=tp, n_past=n_past),
        grid=(b, n // tq),
        in_specs=[q_spec, edge(ATT_WIDTH), edge(ATT_WIDTH), fd_spec, past(kp), past(vp), fp_spec],
        out_specs=q_spec,
        out_shape=jax.ShapeDtypeStruct((b, n, ATT_WIDTH), F32),
        scratch_shapes=[pltpu.VMEM((rows, ATT_WIDTH), BF16), pltpu.VMEM((rows, ATT_WIDTH), F32),
                        pltpu.VMEM((rows, LANES), F32), pltpu.VMEM((rows, LANES), F32)],
        compiler_params=_params("parallel", "arbitrary"),
        name="fox_attn",
    )(q, kd, vd, fcum, kp, vp, fcum)


def _ssd_kernel(xbc_ref, z_ref, sm_ref, smt_ref, cinit_ref, h0_ref, cw_ref, cb_ref, alane_ref, acol_ref,
                dskip_ref, g_ref, y_ref, hfin_ref, xp_ref, h_ref, *, L):
    @pl.when(pl.program_id(1) == 0)
    def _():
        h_ref[...] = h0_ref[0]
        xp_ref[0:CONV_PAD, :] = cinit_ref[0]

    xp_ref[CONV_PAD:CONV_PAD + L, :] = xbc_ref[0]
    first = CONV_PAD - (SSD_CONV - 1)
    conv = cb_ref[...]
    for i in range(SSD_CONV):
        conv = conv + cw_ref[i:i + 1, :] * xp_ref[first + i:first + i + L, :]
    xp_ref[0:CONV_PAD, :] = xp_ref[L:L + CONV_PAD, :]
    xc = _silu(conv)
    xs = xc[:, :SSD_INNER]

    dt_col = sm_ref[0]
    a_col = dt_col * alane_ref[...]
    ri = lax.broadcasted_iota(jnp.int32, (L, L), 0)
    ci = lax.broadcasted_iota(jnp.int32, (L, L), 1)
    causal = ci <= ri
    tri_lower = jnp.where(causal, 1.0, 0.0).astype(BF16)
    tri_upper = jnp.where(ri <= ci, 1.0, 0.0).astype(BF16)
    acum_col = sum(_dot(tri_lower, part) for part in _split3(a_col))
    a_row = smt_ref[0] * acol_ref[...]
    acum_row = _dot(jnp.concatenate(_split3(a_row), axis=1),
                    jnp.concatenate([tri_upper] * 3, axis=0))
    a_last = acum_col[L - 1:L, :]

    ys = []
    heads_per_group = SSD_HEADS // SSD_GROUPS
    for grp in range(SSD_GROUPS):
        b0 = SSD_INNER + grp * SSD_STATE
        c0 = SSD_INNER + SSD_GROUPS * SSD_STATE + grp * SSD_STATE
        bg = xc[:, b0:b0 + SSD_STATE].astype(BF16)
        cg = xc[:, c0:c0 + SSD_STATE].astype(BF16)
        cb = _dot_nt(cg, bg)
        for hh in range(heads_per_group):
            h = grp * heads_per_group + hh
            col = SMALL_DT + h
            ac = acum_col[:, col:col + 1]
            ar = acum_row[col:col + 1, :]
            decay = jnp.exp(jnp.where(causal, ac - ar, NEG))
            scores = (cb * decay).astype(BF16)
            xdt = xs[:, h * SSD_HEAD_DIM:(h + 1) * SSD_HEAD_DIM] * dt_col[:, col:col + 1]
            state = h_ref[h]
            y_h = _dot(scores, xdt.astype(BF16)) + _dot_nt(cg, state.astype(BF16)) * jnp.exp(ac)
            al = a_last[:, col:col + 1]
            xw = (xdt * jnp.exp(al - ac)).astype(BF16)
            h_ref[h] = jnp.exp(al) * state + lax.dot_general(xw, bg, TN_DIMS, preferred_element_type=F32)
            ys.append(y_h)
    y = jnp.concatenate(ys, axis=1) + dskip_ref[...] * xs
    y_ref[0] = _rms(y * _silu(z_ref[0]), g_ref[...])
    hfin_ref[0] = h_ref[...]


def _ssd(xbc, z, small, small_t, conv_init, h0, p, layer, L):
    b, n, _ = xbc.shape
    weights = [p["conv_w"], p["conv_b"], p["a_lane"], p["a_col"], p["dskip_vec"], p["ssd_norm_g"]]
    seq = lambda w: pl.BlockSpec((1, L, w), lambda i, j: (i, j, 0))
    per_b = lambda *s: pl.BlockSpec((1,) + s, lambda i, j: (i,) + (0,) * len(s))
    return pl.pallas_call(
        functools.partial(_ssd_kernel, L=L),
        grid=(b, n // L),
        in_specs=[seq(CONV_DIM), seq(SSD_INNER), seq(LANES),
                  pl.BlockSpec((1, small_t.shape[1], L), lambda i, j: (i, 0, j)),
                  per_b(CONV_PAD, CONV_DIM), per_b(SSD_HEADS, SSD_HEAD_DIM, SSD_STATE)]
        + [_wspec(w, layer) for w in weights],
        out_specs=[seq(SSD_INNER), per_b(SSD_HEADS, SSD_HEAD_DIM, SSD_STATE)],
        out_shape=[jax.ShapeDtypeStruct((b, n, SSD_INNER), F32),
                   jax.ShapeDtypeStruct((b, SSD_HEADS, SSD_HEAD_DIM, SSD_STATE), F32)],
        scratch_shapes=[pltpu.VMEM((L + CONV_PAD, CONV_DIM), F32),
                        pltpu.VMEM((SSD_HEADS, SSD_HEAD_DIM, SSD_STATE), F32)],
        compiler_params=_params("parallel", "arbitrary"),
        name="ssd",
    )(xbc, z, small, small_t, conv_init, h0, *weights)


def _post_mix_kernel(h_ref, sb_ref, fx_ref, ssd_ref, wout_ref, gx_ref, wq_ref, mk_ref, mv_ref, wo_ref, o_ref,
                     h1_ref, q_ref, att_ref, *, rows_per_batch):
    j = pl.program_id(1)
    d = h_ref.shape[1]
    xd = d // X_HEADS

    @pl.when(j == 0)
    def _():
        mix = (_dot(sb_ref[...].astype(BF16), wout_ref[0:ATT_WIDTH, :])
               + _dot(fx_ref[...].astype(BF16), wout_ref[ATT_WIDTH:2 * ATT_WIDTH, :])
               + _dot(ssd_ref[...].astype(BF16), wout_ref[2 * ATT_WIDTH:, :]))
        h1 = h_ref[...] + mix
        h1_ref[...] = h1
        u = _rms(h1, gx_ref[...]).astype(BF16)
        q_ref[...] = (_dot(u, wq_ref[...]) * (LOG2E * xd ** -0.5)).astype(BF16)

    rows = pl.ds(pl.multiple_of(j * rows_per_batch, rows_per_batch), rows_per_batch)
    for hd in range(X_HEADS):
        cols = slice(hd * xd, (hd + 1) * xd)
        s = _dot_nt(q_ref[rows, cols], mk_ref[:, cols].astype(BF16))
        p = jnp.exp2(s - jnp.max(s, axis=1, keepdims=True))
        o = _dot(p.astype(BF16), mv_ref[:, cols].astype(BF16))
        att_ref[rows, cols] = (o * (1.0 / jnp.sum(p, axis=1, keepdims=True))).astype(BF16)

    @pl.when(j == pl.num_programs(1) - 1)
    def _():
        o_ref[...] = h1_ref[...] + _dot(att_ref[...], wo_ref[...])


def _post_mix(h, sb_o, fox_o, ssd_o, p, layer, mk, mv, n, tm):
    m, d = h.shape
    rows_per_batch = min(tm, n)
    row = lambda w: pl.BlockSpec((tm, w), lambda i, j: (i, 0))
    mem_spec = pl.BlockSpec((None, None) + mk.shape[2:], lambda i, j: (layer, (i * tm) // n + j, 0, 0))
    w_out, gx, wq, wo = p["w_out"], p["norm_x_g"], p["wq_x"], p["wo_x"]
    return pl.pallas_call(
        functools.partial(_post_mix_kernel, rows_per_batch=rows_per_batch),
        grid=(m // tm, tm // rows_per_batch),
        in_specs=[row(d), row(ATT_WIDTH), row(ATT_WIDTH), row(SSD_INNER), _wspec(w_out, layer),
                  _wspec(gx, layer), _wspec(wq, layer), mem_spec, mem_spec, _wspec(wo, layer)],
        out_specs=row(d),
        out_shape=jax.ShapeDtypeStruct(h.shape, F32),
        scratch_shapes=[pltpu.VMEM((tm, d), F32), pltpu.VMEM((tm, d), BF16), pltpu.VMEM((tm, d), BF16)],
        compiler_params=_params("parallel", "arbitrary"),
        name="post_mix",
    )(h, sb_o, fox_o, ssd_o, w_out, gx, wq, mk, mv, wo)


def _ffn_kernel(h_ref, g_ref, wg_ref, wu_ref, wd_ref, gf_ref, o_ref, *, final_norm):
    h = h_ref[...]
    u = _rms(h, g_ref[...]).astype(BF16)
    hidden = (_silu(_dot(u, wg_ref[...])) * _dot(u, wu_ref[...])).astype(BF16)
    out = h + _dot(hidden, wd_ref[...])
    if final_norm:
        out = _rms(out, gf_ref[...])
    o_ref[...] = out


def _ffn(h, p, layer, g_final, final_norm, tm):
    m, d = h.shape
    row = pl.BlockSpec((tm, d), lambda i: (i, 0))
    weights = [p["norm_ffn_g"], p["w_gate"], p["w_up"], p["w_down"]]
    return pl.pallas_call(
        functools.partial(_ffn_kernel, final_norm=final_norm),
        grid=(m // tm,),
        in_specs=[row] + [_wspec(w, layer) for w in weights] + [_wspec(g_final)],
        out_specs=row,
        out_shape=jax.ShapeDtypeStruct(h.shape, F32),
        compiler_params=_params("parallel"),
        name="ffn",
    )(h, *weights, g_final)


def _mem_kv_kernel(x_ref, g_ref, wk_ref, wv_ref, k_ref, v_ref, kb_ref, vb_ref):
    u = _rms(x_ref[...], g_ref[...]).astype(BF16)
    xd = k_ref.shape[2]
    for w_ref, o_ref, b_ref in ((wk_ref, k_ref, kb_ref), (wv_ref, v_ref, vb_ref)):
        y = _dot(u, w_ref[...])
        b_ref[...] = y.astype(BF16)
        for hd in range(X_HEADS):
            o_ref[:, hd, :] = y[:, hd * xd:(hd + 1) * xd]


def _mem_kv(mem, p):
    b, t, d = mem.shape
    weights = [p["mem_norm_g"], p["wk_x"], p["wv_x"]]
    layers = weights[0].shape[0]
    heads_block = (t, X_HEADS, d // X_HEADS)
    heads_spec = pl.BlockSpec((None, None) + heads_block, lambda l, i: (l, i, 0, 0, 0))
    rows_spec = pl.BlockSpec((None, None, t, d), lambda l, i: (l, i, 0, 0))
    w_spec = lambda w: pl.BlockSpec((None,) + w.shape[1:], lambda l, i: (l, 0, 0))
    return pl.pallas_call(
        _mem_kv_kernel,
        grid=(layers, b),
        in_specs=[pl.BlockSpec((None, t, d), lambda l, i: (i, 0, 0))] + [w_spec(w) for w in weights],
        out_specs=[heads_spec, heads_spec, rows_spec, rows_spec],
        out_shape=[jax.ShapeDtypeStruct((layers, b) + heads_block, F32)] * 2
        + [jax.ShapeDtypeStruct((layers, b, t, d), BF16)] * 2,
        compiler_params=_params("parallel", "parallel"),
        name="mem_kv",
    )(mem, *weights)


def _seq_minor(a):
    layers, b, n = a.shape[:3]
    return jnp.transpose(a, (0, 1, 3, 4, 2)).reshape(layers, b, -1, n)


def _layer(h, p, layer, past, mem_k, mem_v, final_g, final_norm, kv_stacked=None):
    b, n, d = h.shape
    m = b * n
    prompt = past is None
    tm = min(m, 512)
    outs = _in_proj(h.reshape(m, d), p, layer, tm, b if prompt else None, kv_stacked)
    sq, fq, z, xbc, small = (a.reshape(b, n, a.shape[-1]) for a in outs[:5])
    small_t = outs[5]
    kv_t = outs[6:]
    if not prompt:
        per_batch = lambda a: jnp.transpose(a.reshape(a.shape[0], b, n), (1, 0, 2))
        small_t = per_batch(small_t)
        kv_t = [per_batch(a) for a in kv_t]
    skt, svt, fkt, fvt, skb, svb, fkb, fvb = kv_t
    logf_t = small_t[:, SMALL_LOGF:SMALL_LOGF + ATT_HEADS, :]

    if prompt:
        fcum = _seq_cumsum(small_t[:, :CONV_PAD, :])
        sb_o = _sb_attn(sq, skb, svb, skb, svb)
        fox_o = _fox_attn(fq, fkb, fvb, fkb, fvb, fcum)
        conv_init = jnp.zeros((b, CONV_PAD, CONV_DIM), F32)
        h0 = jnp.zeros((b, SSD_HEADS, SSD_HEAD_DIM, SSD_STATE), F32)
        conv_hist = None
        ssd_len = ATT_BLOCK
    else:
        pad_keys = lambda a: jnp.pad(a, ((0, 0), (0, 0), (0, ATT_BLOCK - n)))
        sb_o = _sb_attn(sq, pad_keys(skb), pad_keys(svb), past["sb_k"], past["sb_v"], layer)
        logf_all = jnp.concatenate([jnp.transpose(past["fox_logf"], (0, 2, 1)), logf_t], axis=2)
        logf_all = jnp.pad(logf_all, ((0, 0), (0, CONV_PAD - ATT_HEADS), (0, ATT_BLOCK - n)))
        fcum = _seq_cumsum(logf_all)
        fox_o = _fox_attn(fq, pad_keys(fkb), pad_keys(fvb), past["fox_k"], past["fox_v"], fcum, layer)
        conv_hist = past["conv"]
        conv_init = jnp.pad(conv_hist, ((0, 0), (CONV_PAD - (SSD_CONV - 1), 0), (0, 0)))
        h0 = past["ssm"]
        ssd_len = n

    ssd_o, h_final = _ssd(xbc, z, small, small_t, conv_init, h0, p, layer, ssd_len)
    if conv_hist is None:
        conv_state = xbc[:, n - (SSD_CONV - 1):, :]
    else:
        conv_state = jnp.concatenate([conv_hist, xbc], axis=1)[:, -(SSD_CONV - 1):, :]

    flat = lambda a: a.reshape(m, a.shape[-1])
    h1 = _post_mix(flat(h), flat(sb_o), flat(fox_o), flat(ssd_o), p, layer, mem_k, mem_v, n, tm)
    h2 = _ffn(h1, p, layer, final_g, final_norm, tm)
    return h2.reshape(b, n, d), (skt, svt, fkt, fvt, logf_t, h_final, conv_state)


def _stacked_params(norm_mix_g, w_in, fox_b_f, conv_w, conv_b, dt_bias, a_log, d_skip, ssd_norm_g, w_out,
                    norm_x_g, mem_norm_g, wq_x, wk_x, wv_x, wo_x, norm_ffn_g, w_gate, w_up, w_down):
    aw = ATT_WIDTH
    qkv = 6 * aw
    gate_end = qkv + ATT_HEADS
    z_end = gate_end + SSD_INNER
    xbc_end = z_end + CONV_DIM
    w = jnp.transpose(w_in, (0, 2, 1))
    layers, _, d = w.shape
    row = lambda a: a.reshape(layers, 1, -1).astype(F32)
    small_pad = LANES - (ATT_HEADS + SSD_HEADS)
    zeros = lambda n: jnp.zeros((layers, n), F32)
    a_small = jnp.concatenate([zeros(ATT_HEADS), -jnp.exp(a_log.astype(F32)), zeros(small_pad)], axis=1)
    b_small = jnp.concatenate([fox_b_f.astype(F32), dt_bias.astype(F32), zeros(small_pad)], axis=1)
    col = lambda a: a[:, :2 * CONV_PAD].reshape(layers, 2 * CONV_PAD, 1)
    bf = lambda a: a.astype(BF16)
    return dict(
        norm_mix_g=row(norm_mix_g),
        wq_t=bf(jnp.concatenate([w[:, 0:aw], w[:, 3 * aw:4 * aw]], axis=1)),
        wkv_t=bf(jnp.concatenate([w[:, aw:3 * aw], w[:, 4 * aw:6 * aw]], axis=1)),
        wzx_t=bf(w[:, gate_end:xbc_end]),
        ws_t=bf(jnp.concatenate([w[:, qkv:gate_end], w[:, xbc_end:], jnp.zeros((layers, small_pad, d), w.dtype)],
                                axis=1)),
        b_small_row=row(b_small), b_small_col=col(b_small),
        conv_w=conv_w.astype(F32), conv_b=row(conv_b),
        a_lane=row(a_small), a_col=col(a_small),
        dskip_vec=row(jnp.repeat(d_skip, SSD_HEAD_DIM, axis=1)), ssd_norm_g=row(ssd_norm_g),
        w_out=bf(w_out), norm_x_g=row(norm_x_g), mem_norm_g=row(mem_norm_g),
        wq_x=bf(wq_x), wk_x=bf(wk_x), wv_x=bf(wv_x), wo_x=bf(wo_x), norm_ffn_g=row(norm_ffn_g),
        w_gate=bf(w_gate), w_up=bf(w_up), w_down=bf(w_down))


def kernel(x_prompt, x_sample, cache_sb_k, cache_sb_v, cache_fox_k, cache_fox_v, cache_fox_logf, state_ssm, state_conv, cache_mem_k, cache_mem_v, mem_prompt, norm_mix_g, w_in, fox_b_f, conv_w, conv_b, dt_bias, a_log, d_skip, ssd_norm_g, w_out, norm_x_g, mem_norm_g, wq_x, wk_x, wv_x, wo_x, norm_ffn_g, w_gate, w_up, w_down, final_norm_g):
    depth = w_in.shape[0]
    bp, mem_len, d = mem_prompt.shape
    bs = x_sample.shape[0]
    final_g = final_norm_g.reshape(1, d).astype(F32)
    p = _stacked_params(norm_mix_g, w_in, fox_b_f, conv_w, conv_b, dt_bias, a_log, d_skip, ssd_norm_g, w_out,
                        norm_x_g, mem_norm_g, wq_x, wk_x, wv_x, wo_x, norm_ffn_g, w_gate, w_up, w_down)
    caches_t = [_seq_minor(c) for c in (cache_sb_k, cache_sb_v, cache_fox_k, cache_fox_v)]
    hp, hs = x_prompt, x_sample
    p_mk, p_mv, p_mk_rows, p_mv_rows = _mem_kv(mem_prompt, p)
    s_mk_rows = cache_mem_k.reshape(depth, bs, mem_len, d)
    s_mv_rows = cache_mem_v.reshape(depth, bs, mem_len, d)
    p_st, s_st = [], []
    p_kv = None
    for l in range(depth):
        last = l == depth - 1
        hp, st = _layer(hp, p, l, None, p_mk_rows, p_mv_rows, final_g, last, p_kv)
        p_kv = st[:4]
        p_st.append(st)
        past = dict(sb_k=caches_t[0], sb_v=caches_t[1], fox_k=caches_t[2], fox_v=caches_t[3],
                    fox_logf=cache_fox_logf[l], ssm=state_ssm[l], conv=state_conv[l])
        hs, st = _layer(hs, p, l, past, s_mk_rows, s_mv_rows, final_g, last)
        s_st.append(st)
    stk = lambda sts, i: jnp.stack([s[i] for s in sts], axis=0)
    kv = lambda a: jnp.transpose(a.reshape(depth, a.shape[1], ATT_HEADS, HEAD_DIM, -1), (0, 1, 4, 2, 3))
    logf = lambda sts: jnp.transpose(stk(sts, 4), (0, 1, 3, 2))
    return (hp, hs,
            kv(p_kv[0]), kv(p_kv[1]), kv(p_kv[2]), kv(p_kv[3]), logf(p_st), stk(p_st, 5), stk(p_st, 6),
            p_mk, p_mv,
            kv(stk(s_st, 0)), kv(stk(s_st, 1)), kv(stk(s_st, 2)), kv(stk(s_st, 3)), logf(s_st),
            stk(s_st, 5), stk(s_st, 6))
```

```python
import functools

import jax
import jax.numpy as jnp
from jax import lax
from jax.experimental import pallas as pl
from jax.experimental.pallas import tpu as pltpu

F32 = jnp.float32
BF16 = jnp.bfloat16
EPS = 1e-6

HEAD_DIM = 64
ATT_HEADS = 4
ATT_WIDTH = ATT_HEADS * HEAD_DIM
SSD_HEADS = 8
SSD_HEAD_DIM = 64
SSD_INNER = SSD_HEADS * SSD_HEAD_DIM
SSD_GROUPS = 2
SSD_STATE = 128
SSD_CONV = 4
CONV_DIM = SSD_INNER + 2 * SSD_GROUPS * SSD_STATE
X_HEADS = 4
LANES = 128
SMALL_LOGF = 0
SMALL_DT = ATT_HEADS
CONV_PAD = 8
ATT_BLOCK = 256
SAMPLE_PAST_BLOCK = 1024
NEG = -1e30
LOG2E = 1.4426950408889634
SB_DEAD_LOG2 = 150.0
SB_RECENT_BLOCKS = 2
VMEM_LIMIT = 56 * 1024 * 1024

NT_DIMS = (((1,), (1,)), ((), ()))
TN_DIMS = (((0,), (0,)), ((), ()))


def _params(*sem):
    return pltpu.CompilerParams(dimension_semantics=sem, vmem_limit_bytes=VMEM_LIMIT)


def _rms(x, g):
    ms = jnp.mean(x * x, axis=-1, keepdims=True)
    return x * lax.rsqrt(ms + EPS) * g


def _softplus(x):
    t = jnp.exp2(jnp.abs(x) * (-LOG2E))
    return jnp.maximum(x, 0.0) + jnp.log(1.0 + t)


def _softplus2(x):
    return jnp.maximum(x, 0.0) + jnp.log2(1.0 + jnp.exp2(-jnp.abs(x)))


def _silu(x):
    half = 0.5 * x
    return half + half * jnp.tanh(half)


def _dot(a, b):
    return jnp.dot(a, b, preferred_element_type=F32)


def _dot_nt(a, b):
    return lax.dot_general(a, b, NT_DIMS, preferred_element_type=F32)


def _split3(x):
    hi = x.astype(BF16)
    r1 = x - hi.astype(F32)
    mid = r1.astype(BF16)
    lo = (r1 - mid.astype(F32)).astype(BF16)
    return hi, mid, lo


def _wspec(a, layer=None):
    once = pl.Buffered(1)
    if layer is None:
        return pl.BlockSpec(a.shape, lambda *_: (0,) * a.ndim, pipeline_mode=once)
    return pl.BlockSpec((None,) + a.shape[1:], lambda *_: (layer,) + (0,) * (a.ndim - 1), pipeline_mode=once)


def _in_proj_kernel(*refs, n_prev):
    x_ref, g_ref, wq_ref, wkv_ref, wzx_ref, ws_ref, bs_row_ref, bs_col_ref = refs[:8]
    prev_refs = refs[8:8 + n_prev]
    (sq_ref, fq_ref, z_ref, xbc_ref, small_ref, small_t_ref,
     skt_ref, svt_ref, fkt_ref, fvt_ref, skb_ref, svb_ref, fkb_ref, fvb_ref) = refs[8 + n_prev:]
    u = _rms(x_ref[...], g_ref[...]).astype(BF16)
    q = _dot_nt(u, wq_ref[...]) * (LOG2E * HEAD_DIM ** -0.5)
    sq_ref[...] = q[:, :ATT_WIDTH].astype(BF16)
    fq_ref[...] = q[:, ATT_WIDTH:].astype(BF16)
    zx = _dot_nt(u, wzx_ref[...])
    z_ref[...] = zx[:, :SSD_INNER]
    xbc_ref[...] = zx[:, SSD_INNER:]
    kvt = _dot_nt(wkv_ref[...], u)
    for i, (f_ref, b_ref) in enumerate(((skt_ref, skb_ref), (svt_ref, svb_ref),
                                        (fkt_ref, fkb_ref), (fvt_ref, fvb_ref))):
        part = kvt[i * ATT_WIDTH:(i + 1) * ATT_WIDTH, :]
        if f_ref.ndim == 3:
            if n_prev:
                f_ref[0:f_ref.shape[0] - 1] = prev_refs[i][...]
            f_ref[f_ref.shape[0] - 1] = part
        else:
            f_ref[...] = part
        b_ref[...] = part.astype(BF16).reshape(b_ref.shape)

    def gates(s, index):
        is_gate = index < SMALL_DT
        sp = _softplus(jnp.where(is_gate, -s, s))
        return jnp.where(is_gate, -sp, sp)

    s = _dot_nt(u, ws_ref[...]) + bs_row_ref[...]
    small_ref[...] = gates(s, lax.broadcasted_iota(jnp.int32, s.shape, 1))
    st = kvt[4 * ATT_WIDTH:, :] + bs_col_ref[...]
    small_t_ref[...] = gates(st, lax.broadcasted_iota(jnp.int32, st.shape, 0)).reshape(small_t_ref.shape)


def _in_proj(x, p, layer, tm, batch, stacked=None):
    m, d = x.shape
    row = lambda n: pl.BlockSpec((tm, n), lambda i: (i, 0))
    weights = [p["norm_mix_g"], p["wq_t"], p["wkv_t"], p["wzx_t"], p["ws_t"], p["b_small_row"], p["b_small_col"]]
    rows = [(ATT_WIDTH, BF16), (ATT_WIDTH, BF16), (SSD_INNER, F32), (CONV_DIM, F32), (LANES, F32)]
    stacked = list(stacked or [])
    if batch is None:
        t_shape = kv_shape = lambda r: (r, m)
        t_spec = kv_spec = lambda r: pl.BlockSpec((r, tm), lambda i: (0, i))
        prev_specs = []
    else:
        nb = m // batch // tm
        t_shape = lambda r: (batch, r, m // batch)
        t_spec = lambda r: pl.BlockSpec((1, r, tm), lambda i: (i // nb, 0, i % nb))
        stack_spec = lambda n, r: pl.BlockSpec((n, None, r, tm), lambda i: (0, i // nb, 0, i % nb))
        kv_shape = lambda r: (layer + 1, batch, r, m // batch)
        kv_spec = lambda r: stack_spec(layer + 1, r)
        prev_specs = [stack_spec(layer, ATT_WIDTH)] * len(stacked)
    out_specs = ([row(n) for n, _ in rows] + [t_spec(2 * CONV_PAD)] + [kv_spec(ATT_WIDTH)] * 4
                 + [t_spec(ATT_WIDTH)] * 4)
    out_shape = ([jax.ShapeDtypeStruct((m, n), dt) for n, dt in rows]
                 + [jax.ShapeDtypeStruct(t_shape(2 * CONV_PAD), F32)]
                 + [jax.ShapeDtypeStruct(kv_shape(ATT_WIDTH), F32)] * 4
                 + [jax.ShapeDtypeStruct(t_shape(ATT_WIDTH), BF16)] * 4)
    return pl.pallas_call(
        functools.partial(_in_proj_kernel, n_prev=len(stacked)),
        grid=(m // tm,),
        in_specs=[row(d)] + [_wspec(w, layer) for w in weights] + prev_specs,
        out_specs=out_specs,
        out_shape=out_shape,
        compiler_params=_params("parallel"),
        name="in_proj",
    )(x, *weights, *stacked)


def _cumsum_kernel(x_ref, o_ref, *, blk):
    n = x_ref.shape[2]
    r = lax.broadcasted_iota(jnp.int32, (blk, blk), 0)
    c = lax.broadcasted_iota(jnp.int32, (blk, blk), 1)
    tri = jnp.where(r <= c, 1.0, 0.0).astype(BF16)
    tri3 = jnp.concatenate([tri, tri, tri], axis=0)
    carry = jnp.zeros((x_ref.shape[1], 1), F32)
    for j in range(n // blk):
        xb = x_ref[0, :, j * blk:(j + 1) * blk]
        out = _dot(jnp.concatenate(_split3(xb), axis=1), tri3) + carry
        o_ref[0, :, j * blk:(j + 1) * blk] = out
        carry = out[:, blk - 1:blk]


def _seq_cumsum(x):
    b, r, n = x.shape
    spec = pl.BlockSpec((1, b * r, n), lambda i: (0, 0, 0))
    return pl.pallas_call(
        functools.partial(_cumsum_kernel, blk=ATT_BLOCK),
        grid=(1,),
        in_specs=[spec],
        out_specs=spec,
        out_shape=jax.ShapeDtypeStruct((1, b * r, n), F32),
        compiler_params=_params("arbitrary"),
        name="seq_cumsum",
    )(x.reshape(1, b * r, n)).reshape(b, r, n)


def _stack_heads(q, qbd_ref, tq):
    lane_head = lax.broadcasted_iota(jnp.int32, q.shape, 1) // HEAD_DIM
    for h in range(ATT_HEADS):
        qbd_ref[h * tq:(h + 1) * tq, :] = jnp.where(lane_head == h, q, jnp.zeros_like(q))


def _unstack_heads(acc, tq):
    lane_head = lax.broadcasted_iota(jnp.int32, (tq, ATT_WIDTH), 1) // HEAD_DIM
    out = jnp.zeros((tq, ATT_WIDTH), F32)
    for h in range(ATT_HEADS):
        out = jnp.where(lane_head == h, acc[h * tq:(h + 1) * tq, :], out)
    return out


def _local_positions(rows, tq, tk):
    t_loc = lax.broadcasted_iota(jnp.int32, (rows, tk), 0) & (tq - 1)
    s_loc = lax.broadcasted_iota(jnp.int32, (rows, tk), 1)
    return t_loc, s_loc


def _lane_tile(x, width):
    return jnp.concatenate([x] * (width // LANES), axis=1)


def _past_loop(run_at, n_blocks, tp, keep_going=None):
    def run(step):
        run_at(pl.multiple_of((n_blocks - 1 - step) * tp, tp))

    if keep_going is None:
        def body(step, carry):
            run(step)
            return carry

        lax.fori_loop(0, n_blocks, body, 0)
    else:
        def body(state):
            run(state[0])
            return state[0] + 1, keep_going()

        lax.while_loop(lambda state: (state[0] < n_blocks) & (state[1] > 0), body,
                       (jnp.int32(0), jnp.int32(1)))


def _attn_specs(tq, td, prompt, layer):
    q_spec = pl.BlockSpec((None, tq, ATT_WIDTH), lambda i, j: (i, j, 0))
    if prompt:
        edge = lambda r: pl.BlockSpec((None, r, td), lambda i, j: (i, 0, j))
        past = lambda a: pl.BlockSpec((None,) + a.shape[1:], lambda i, j: (i, 0, 0))
    else:
        edge = lambda r: pl.BlockSpec((None, r, td), lambda i, j: (i, 0, 0))
        past = lambda a: pl.BlockSpec((None, None) + a.shape[2:], lambda i, j: (layer, i, 0, 0))
    return q_spec, edge, past


def _sb_kernel(*refs, tq, tp, n_past, n_recent, layer):
    n_past_inputs = 2 if n_past is None else 4
    q_ref, kd_ref, vd_ref = refs[:3]
    o_ref = refs[3 + n_past_inputs]
    qbd_ref, acc_ref, carry_ref = refs[4 + n_past_inputs:7 + n_past_inputs]
    past_refs = refs[3:3 + n_past_inputs] + refs[7 + n_past_inputs:]
    rows = ATT_HEADS * tq
    _stack_heads(q_ref[...], qbd_ref, tq)
    acc_ref[...] = jnp.zeros_like(acc_ref)
    carry_ref[...] = jnp.zeros_like(carry_ref)
    r = lax.broadcasted_iota(jnp.int32, (2 * LANES, 2 * LANES), 0) & (LANES - 1)
    c = lax.broadcasted_iota(jnp.int32, (2 * LANES, 2 * LANES), 1)
    wcum = jnp.where((c >= LANES) | (r > c), -1.0, 0.0).astype(BF16)

    def block(kt, vt, diagonal=False):
        kt = kt.astype(BF16)
        vt = vt.astype(BF16)
        tk = kt.shape[1]
        s = _dot(qbd_ref[...], kt)
        sp = _softplus2(s)
        log_beta = s - sp
        if diagonal:
            t_loc, s_loc = _local_positions(rows, tq, tk)
            valid = s_loc < t_loc
            sp = jnp.where(valid, sp, 0.0)
        hi = sp.astype(BF16)
        lo = (sp - hi.astype(F32)).astype(BF16)
        carry = carry_ref[...]
        after = [None] * (tk // LANES)
        for g in reversed(range(tk // LANES)):
            cols = slice(g * LANES, (g + 1) * LANES)
            a = _dot(jnp.concatenate([hi[:, cols], lo[:, cols]], axis=1), wcum)
            after[g] = a[:, :LANES] + carry
            carry = carry + a[:, LANES:]
        carry_ref[...] = carry
        w = jnp.exp2(log_beta + jnp.concatenate(after, axis=1))
        if diagonal:
            w = jnp.where(valid, w, 0.0)
        acc_ref[...] += _dot_nt(w.astype(BF16), vt)

    def any_weight_left():
        return (jnp.max(carry_ref[...]) > -SB_DEAD_LOG2).astype(jnp.int32)

    block(kd_ref[...], vd_ref[...], diagonal=True)
    if n_past is None:
        kp_ref, vp_ref = past_refs

        def run_at(start):
            block(kp_ref[:, pl.ds(start, tp)], vp_ref[:, pl.ds(start, tp)])

        n_blocks = pl.program_id(1)
    else:
        kr_ref, vr_ref, k_all, v_all, kbuf_ref, vbuf_ref, sem = past_refs
        recent_start = (n_past - n_recent) * tp
        batch = pl.program_id(0)

        def run_at(start):
            @pl.when(start >= recent_start)
            def _():
                off = pl.multiple_of(start - recent_start, tp)
                block(kr_ref[:, pl.ds(off, tp)], vr_ref[:, pl.ds(off, tp)])

            @pl.when(start < recent_start)
            def _():
                copies = [pltpu.make_async_copy(src.at[layer, batch, :, pl.ds(start, tp)], dst, sem.at[i])
                          for i, (src, dst) in enumerate(((k_all, kbuf_ref), (v_all, vbuf_ref)))]
                for c in copies:
                    c.start()
                for c in copies:
                    c.wait()
                block(kbuf_ref[...], vbuf_ref[...])

        n_blocks = n_past
    _past_loop(run_at, n_blocks, tp, keep_going=any_weight_left)
    o_ref[...] = _unstack_heads(acc_ref[...], tq)


def _sb_attn(q, kd, vd, kp, vp, layer=None):
    b, n, _ = q.shape
    prompt = layer is None
    td = ATT_BLOCK
    tq = ATT_BLOCK if prompt else n
    rows = ATT_HEADS * tq
    q_spec, edge, past = _attn_specs(tq, td, prompt, layer)
    scratch = [pltpu.VMEM((rows, ATT_WIDTH), BF16), pltpu.VMEM((rows, ATT_WIDTH), F32),
               pltpu.VMEM((rows, LANES), F32)]
    if prompt:
        n_past = None
        past_specs = [past(kp), past(vp)]
        past_args = (kp, vp)
    else:
        n_past = kp.shape[-1] // td
        n_recent = min(SB_RECENT_BLOCKS, n_past)
        last_window = n_past // n_recent - 1
        recent = pl.BlockSpec((None, None, ATT_WIDTH, n_recent * td), lambda i, j: (layer, i, 0, last_window))
        whole = pl.BlockSpec(memory_space=pl.ANY)
        past_specs = [recent, recent, whole, whole]
        past_args = (kp, vp, kp, vp)
        scratch += [pltpu.VMEM((ATT_WIDTH, td), kp.dtype), pltpu.VMEM((ATT_WIDTH, td), vp.dtype),
                    pltpu.SemaphoreType.DMA((2,))]
    return pl.pallas_call(
        functools.partial(_sb_kernel, tq=tq, tp=td, n_past=n_past,
                          n_recent=None if prompt else n_recent, layer=layer),
        grid=(b, n // tq),
        in_specs=[q_spec, edge(ATT_WIDTH), edge(ATT_WIDTH)] + past_specs,
        out_specs=q_spec,
        out_shape=jax.ShapeDtypeStruct((b, n, ATT_WIDTH), F32),
        scratch_shapes=scratch,
        compiler_params=_params("parallel", "arbitrary"),
        name="sb_attn",
    )(q, kd, vd, *past_args)


def _fox_kernel(q_ref, kd_ref, vd_ref, fd_ref, kp_ref, vp_ref, fp_ref, o_ref,
                qbd_ref, acc_ref, m_ref, l_ref, *, tq, td, tp, n_past):
    rows = ATT_HEADS * tq
    _stack_heads(q_ref[...], qbd_ref, tq)
    acc_ref[...] = jnp.zeros_like(acc_ref)
    m_ref[...] = jnp.full_like(m_ref, NEG)
    l_ref[...] = jnp.zeros_like(l_ref)

    def block(kt, vt, fk, diagonal=False):
        kt = kt.astype(BF16)
        vt = vt.astype(BF16)
        fk = fk * LOG2E
        tk = kt.shape[1]
        s = _dot(qbd_ref[...], kt)
        sh = jnp.concatenate([s[h * tq:(h + 1) * tq] - fk[h:h + 1, :] for h in range(ATT_HEADS)], axis=0)
        if diagonal:
            t_loc, s_loc = _local_positions(rows, tq, tk)
            sh = jnp.where(s_loc <= t_loc, sh, NEG)
        m_old = m_ref[...]
        m_new = jnp.maximum(m_old, jnp.max(sh, axis=1, keepdims=True))
        alpha = jnp.exp2(m_old - m_new)
        p = jnp.exp2(sh - _lane_tile(m_new, tk))
        l_ref[...] = alpha * l_ref[...] + jnp.sum(p, axis=1, keepdims=True)
        m_ref[...] = m_new
        acc_ref[...] = acc_ref[...] * _lane_tile(alpha, ATT_WIDTH) + _dot_nt(p.astype(BF16), vt)

    block(kd_ref[...], vd_ref[...], fd_ref[...], diagonal=True)
    def run(start, size):
        block(kp_ref[:, pl.ds(start, size)], vp_ref[:, pl.ds(start, size)], fp_ref[:, pl.ds(start, size)])

    if n_past is None:
        n_edge = pl.program_id(1)

        @pl.when(n_edge % 2 == 1)
        def _():
            run(pl.multiple_of((n_edge - 1) * td, td), td)

        _past_loop(functools.partial(run, size=tp), n_edge // 2, tp)
    else:
        _past_loop(functools.partial(run, size=tp), n_past, tp)
    o_ref[...] = _unstack_heads(acc_ref[...] * _lane_tile(1.0 / l_ref[...], ATT_WIDTH), tq)


def _fox_attn(q, kd, vd, kp, vp, fcum, layer=None):
    b, n, _ = q.shape
    prompt = layer is None
    td = ATT_BLOCK
    tq = ATT_BLOCK if prompt else n
    tp = 2 * ATT_BLOCK if prompt else SAMPLE_PAST_BLOCK
    rows = ATT_HEADS * tq
    q_spec, edge, past = _attn_specs(tq, td, prompt, layer)
    n_past = None if prompt else kp.shape[-1] // tp
    fr = fcum.shape[1]
    if prompt:
        fd_spec = edge(fr)
    else:
        last = kp.shape[-1] // td
        fd_spec = pl.BlockSpec((None, fr, td), lambda i, j: (i, 0, last))
    fp_spec = pl.BlockSpec((None,) + fcum.shape[1:], lambda i, j: (i, 0, 0))
    return pl.pallas_call(
        functools.partial(_fox_kernel, tq=tq, td=td, tp=tp, n_past=n_past),
        grid=(b, n // tq),
        in_specs=[q_spec, edge(ATT_WIDTH), edge(ATT_WIDTH), fd_spec, past(kp), past(vp), fp_spec],
        out_specs=q_spec,
        out_shape=jax.ShapeDtypeStruct((b, n, ATT_WIDTH), F32),
        scratch_shapes=[pltpu.VMEM((rows, ATT_WIDTH), BF16), pltpu.VMEM((rows, ATT_WIDTH), F32),
                        pltpu.VMEM((rows, LANES), F32), pltpu.VMEM((rows, LANES), F32)],
        compiler_params=_params("parallel", "arbitrary"),
        name="fox_attn",
    )(q, kd, vd, fcum, kp, vp, fcum)


def _ssd_kernel(xbc_ref, z_ref, sm_ref, smt_ref, cinit_ref, h0_ref, cw_ref, cb_ref, alane_ref, acol_ref,
                dskip_ref, g_ref, y_ref, hfin_ref, xp_ref, h_ref, *, L):
    @pl.when(pl.program_id(1) == 0)
    def _():
        h_ref[...] = h0_ref[0]
        xp_ref[0:CONV_PAD, :] = cinit_ref[0]

    xp_ref[CONV_PAD:CONV_PAD + L, :] = xbc_ref[0]
    first = CONV_PAD - (SSD_CONV - 1)
    conv = cb_ref[...]
    for i in range(SSD_CONV):
        conv = conv + cw_ref[i:i + 1, :] * xp_ref[first + i:first + i + L, :]
    xp_ref[0:CONV_PAD, :] = xp_ref[L:L + CONV_PAD, :]
    xc = _silu(conv)
    xs = xc[:, :SSD_INNER]

    dt_col = sm_ref[0]
    a_col = dt_col * alane_ref[...]
    ri = lax.broadcasted_iota(jnp.int32, (L, L), 0)
    ci = lax.broadcasted_iota(jnp.int32, (L, L), 1)
    causal = ci <= ri
    tri_lower = jnp.where(causal, 1.0, 0.0).astype(BF16)
    tri_upper = jnp.where(ri <= ci, 1.0, 0.0).astype(BF16)
    acum_col = sum(_dot(tri_lower, part) for part in _split3(a_col))
    a_row = smt_ref[0] * acol_ref[...]
    acum_row = _dot(jnp.concatenate(_split3(a_row), axis=1),
                    jnp.concatenate([tri_upper] * 3, axis=0))
    a_last = acum_col[L - 1:L, :]

    e_row = lax.broadcasted_iota(jnp.int32, (LANES, SSD_INNER), 0)
    e_head = lax.broadcasted_iota(jnp.int32, (LANES, SSD_INNER), 1) // SSD_HEAD_DIM
    expand = jnp.where(e_row == e_head + SMALL_DT, 1.0, 0.0).astype(BF16)

    def per_channel(x, n_parts):
        parts = _split3(x)[:n_parts]
        return _dot(jnp.concatenate(parts, axis=1), jnp.concatenate([expand] * n_parts, axis=0))

    ac_all = per_channel(acum_col, 3)
    xdt_all = xs * per_channel(dt_col, 2)
    xw_all = xdt_all * jnp.exp(ac_all[L - 1:L, :] - ac_all)
    y_scale = jnp.exp(ac_all)

    pair = 2 * SSD_HEAD_DIM
    first_of_pair = lax.broadcasted_iota(jnp.int32, (L, pair), 1) < SSD_HEAD_DIM
    first_rows = lax.broadcasted_iota(jnp.int32, (pair, SSD_STATE), 0) < SSD_HEAD_DIM
    ys = []
    pairs_per_group = SSD_HEADS // SSD_GROUPS // 2
    for grp in range(SSD_GROUPS):
        b0 = SSD_INNER + grp * SSD_STATE
        c0 = SSD_INNER + SSD_GROUPS * SSD_STATE + grp * SSD_STATE
        bg = xc[:, b0:b0 + SSD_STATE].astype(BF16)
        cg = xc[:, c0:c0 + SSD_STATE].astype(BF16)
        cb = _dot_nt(cg, bg)
        for pp in range(pairs_per_group):
            h0 = 2 * (grp * pairs_per_group + pp)
            cols = slice(h0 * SSD_HEAD_DIM, (h0 + 2) * SSD_HEAD_DIM)
            xdt_pair = xdt_all[:, cols]
            state = h_ref[h0:h0 + 2].reshape(pair, SSD_STATE)
            y_pair = _dot_nt(cg, state.astype(BF16)) * y_scale[:, cols]
            for k in range(2):
                col = SMALL_DT + h0 + k
                ac = acum_col[:, col:col + 1]
                ar = acum_row[col:col + 1, :]
                decay = jnp.exp(jnp.where(causal, ac - ar, NEG))
                scores = (cb * decay).astype(BF16)
                own = first_of_pair if k == 0 else ~first_of_pair
                y_pair = y_pair + _dot(scores, jnp.where(own, xdt_pair, 0.0).astype(BF16))
            block_decay = jnp.where(first_rows, jnp.exp(a_last[:, SMALL_DT + h0:SMALL_DT + h0 + 1]),
                                    jnp.exp(a_last[:, SMALL_DT + h0 + 1:SMALL_DT + h0 + 2]))
            new_state = block_decay * state + lax.dot_general(xw_all[:, cols].astype(BF16), bg, TN_DIMS,
                                                              preferred_element_type=F32)
            h_ref[h0:h0 + 2] = new_state.reshape(2, SSD_HEAD_DIM, SSD_STATE)
            ys.append(y_pair)
    y = jnp.concatenate(ys, axis=1) + dskip_ref[...] * xs
    y_ref[0] = _rms(y * _silu(z_ref[0]), g_ref[...])
    hfin_ref[0] = h_ref[...]


def _ssd(xbc, z, small, small_t, conv_init, h0, p, layer, L):
    b, n, _ = xbc.shape
    weights = [p["conv_w"], p["conv_b"], p["a_lane"], p["a_col"], p["dskip_vec"], p["ssd_norm_g"]]
    seq = lambda w: pl.BlockSpec((1, L, w), lambda i, j: (i, j, 0))
    per_b = lambda *s: pl.BlockSpec((1,) + s, lambda i, j: (i,) + (0,) * len(s))
    return pl.pallas_call(
        functools.partial(_ssd_kernel, L=L),
        grid=(b, n // L),
        in_specs=[seq(CONV_DIM), seq(SSD_INNER), seq(LANES),
                  pl.BlockSpec((1, small_t.shape[1], L), lambda i, j: (i, 0, j)),
                  per_b(CONV_PAD, CONV_DIM), per_b(SSD_HEADS, SSD_HEAD_DIM, SSD_STATE)]
        + [_wspec(w, layer) for w in weights],
        out_specs=[seq(SSD_INNER), per_b(SSD_HEADS, SSD_HEAD_DIM, SSD_STATE)],
        out_shape=[jax.ShapeDtypeStruct((b, n, SSD_INNER), F32),
                   jax.ShapeDtypeStruct((b, SSD_HEADS, SSD_HEAD_DIM, SSD_STATE), F32)],
        scratch_shapes=[pltpu.VMEM((L + CONV_PAD, CONV_DIM), F32),
                        pltpu.VMEM((SSD_HEADS, SSD_HEAD_DIM, SSD_STATE), F32)],
        compiler_params=_params("parallel", "arbitrary"),
        name="ssd",
    )(xbc, z, small, small_t, conv_init, h0, *weights)


def _post_mix_kernel(h_ref, sb_ref, fx_ref, ssd_ref, wout_ref, gx_ref, wq_ref, mk_ref, mv_ref, wo_ref, o_ref,
                     h1_ref, q_ref, att_ref, *, rows_per_batch):
    j = pl.program_id(1)
    d = h_ref.shape[1]
    xd = d // X_HEADS

    @pl.when(j == 0)
    def _():
        mix = (_dot(sb_ref[...].astype(BF16), wout_ref[0:ATT_WIDTH, :])
               + _dot(fx_ref[...].astype(BF16), wout_ref[ATT_WIDTH:2 * ATT_WIDTH, :])
               + _dot(ssd_ref[...].astype(BF16), wout_ref[2 * ATT_WIDTH:, :]))
        h1 = h_ref[...] + mix
        h1_ref[...] = h1
        u = _rms(h1, gx_ref[...]).astype(BF16)
        q_ref[...] = (_dot(u, wq_ref[...]) * (LOG2E * xd ** -0.5)).astype(BF16)

    rows = pl.ds(pl.multiple_of(j * rows_per_batch, rows_per_batch), rows_per_batch)
    for hd in range(X_HEADS):
        cols = slice(hd * xd, (hd + 1) * xd)
        s = _dot_nt(q_ref[rows, cols], mk_ref[:, cols].astype(BF16))
        p = jnp.exp2(s - jnp.max(s, axis=1, keepdims=True))
        o = _dot(p.astype(BF16), mv_ref[:, cols].astype(BF16))
        att_ref[rows, cols] = (o * (1.0 / jnp.sum(p, axis=1, keepdims=True))).astype(BF16)

    @pl.when(j == pl.num_programs(1) - 1)
    def _():
        o_ref[...] = h1_ref[...] + _dot(att_ref[...], wo_ref[...])


def _post_mix(h, sb_o, fox_o, ssd_o, p, layer, mk, mv, n, tm):
    m, d = h.shape
    rows_per_batch = min(tm, n)
    row = lambda w: pl.BlockSpec((tm, w), lambda i, j: (i, 0))
    mem_spec = pl.BlockSpec((None, None) + mk.shape[2:], lambda i, j: (layer, (i * tm) // n + j, 0, 0))
    w_out, gx, wq, wo = p["w_out"], p["norm_x_g"], p["wq_x"], p["wo_x"]
    return pl.pallas_call(
        functools.partial(_post_mix_kernel, rows_per_batch=rows_per_batch),
        grid=(m // tm, tm // rows_per_batch),
        in_specs=[row(d), row(ATT_WIDTH), row(ATT_WIDTH), row(SSD_INNER), _wspec(w_out, layer),
                  _wspec(gx, layer), _wspec(wq, layer), mem_spec, mem_spec, _wspec(wo, layer)],
        out_specs=row(d),
        out_shape=jax.ShapeDtypeStruct(h.shape, F32),
        scratch_shapes=[pltpu.VMEM((tm, d), F32), pltpu.VMEM((tm, d), BF16), pltpu.VMEM((tm, d), BF16)],
        compiler_params=_params("parallel", "arbitrary"),
        name="post_mix",
    )(h, sb_o, fox_o, ssd_o, w_out, gx, wq, mk, mv, wo)


def _ffn_kernel(h_ref, g_ref, wg_ref, wu_ref, wd_ref, gf_ref, o_ref, *, final_norm):
    h = h_ref[...]
    u = _rms(h, g_ref[...]).astype(BF16)
    hidden = (_silu(_dot(u, wg_ref[...])) * _dot(u, wu_ref[...])).astype(BF16)
    out = h + _dot(hidden, wd_ref[...])
    if final_norm:
        out = _rms(out, gf_ref[...])
    o_ref[...] = out


def _ffn(h, p, layer, g_final, final_norm, tm):
    m, d = h.shape
    row = pl.BlockSpec((tm, d), lambda i: (i, 0))
    weights = [p["norm_ffn_g"], p["w_gate"], p["w_up"], p["w_down"]]
    return pl.pallas_call(
        functools.partial(_ffn_kernel, final_norm=final_norm),
        grid=(m // tm,),
        in_specs=[row] + [_wspec(w, layer) for w in weights] + [_wspec(g_final)],
        out_specs=row,
        out_shape=jax.ShapeDtypeStruct(h.shape, F32),
        compiler_params=_params("parallel"),
        name="ffn",
    )(h, *weights, g_final)


def _mem_kv_kernel(x_ref, g_ref, wk_ref, wv_ref, k_ref, v_ref, kb_ref, vb_ref):
    u = _rms(x_ref[...], g_ref[...]).astype(BF16)
    xd = k_ref.shape[2]
    for w_ref, o_ref, b_ref in ((wk_ref, k_ref, kb_ref), (wv_ref, v_ref, vb_ref)):
        y = _dot(u, w_ref[...])
        b_ref[...] = y.astype(BF16)
        for hd in range(X_HEADS):
            o_ref[:, hd, :] = y[:, hd * xd:(hd + 1) * xd]


def _mem_kv(mem, p):
    b, t, d = mem.shape
    weights = [p["mem_norm_g"], p["wk_x"], p["wv_x"]]
    layers = weights[0].shape[0]
    heads_block = (t, X_HEADS, d // X_HEADS)
    heads_spec = pl.BlockSpec((None, None) + heads_block, lambda l, i: (l, i, 0, 0, 0))
    rows_spec = pl.BlockSpec((None, None, t, d), lambda l, i: (l, i, 0, 0))
    w_spec = lambda w: pl.BlockSpec((None,) + w.shape[1:], lambda l, i: (l, 0, 0))
    return pl.pallas_call(
        _mem_kv_kernel,
        grid=(layers, b),
        in_specs=[pl.BlockSpec((None, t, d), lambda l, i: (i, 0, 0))] + [w_spec(w) for w in weights],
        out_specs=[heads_spec, heads_spec, rows_spec, rows_spec],
        out_shape=[jax.ShapeDtypeStruct((layers, b) + heads_block, F32)] * 2
        + [jax.ShapeDtypeStruct((layers, b, t, d), BF16)] * 2,
        compiler_params=_params("parallel", "parallel"),
        name="mem_kv",
    )(mem, *weights)


def _seq_minor(a):
    layers, b, n = a.shape[:3]
    return jnp.transpose(a, (0, 1, 3, 4, 2)).reshape(layers, b, -1, n)


def _layer(h, p, layer, past, mem_k, mem_v, final_g, final_norm, kv_stacked=None):
    b, n, d = h.shape
    m = b * n
    prompt = past is None
    tm = min(m, 512)
    outs = _in_proj(h.reshape(m, d), p, layer, tm, b if prompt else None, kv_stacked)
    sq, fq, z, xbc, small = (a.reshape(b, n, a.shape[-1]) for a in outs[:5])
    small_t = outs[5]
    kv_t = outs[6:]
    if not prompt:
        per_batch = lambda a: jnp.transpose(a.reshape(a.shape[0], b, n), (1, 0, 2))
        small_t = per_batch(small_t)
        kv_t = [per_batch(a) for a in kv_t]
    skt, svt, fkt, fvt, skb, svb, fkb, fvb = kv_t
    logf_t = small_t[:, SMALL_LOGF:SMALL_LOGF + ATT_HEADS, :]

    if prompt:
        fcum = _seq_cumsum(small_t[:, :CONV_PAD, :])
        sb_o = _sb_attn(sq, skb, svb, skb, svb)
        fox_o = _fox_attn(fq, fkb, fvb, fkb, fvb, fcum)
        conv_init = jnp.zeros((b, CONV_PAD, CONV_DIM), F32)
        h0 = jnp.zeros((b, SSD_HEADS, SSD_HEAD_DIM, SSD_STATE), F32)
        conv_hist = None
        ssd_len = ATT_BLOCK
    else:
        pad_keys = lambda a: jnp.pad(a, ((0, 0), (0, 0), (0, ATT_BLOCK - n)))
        sb_o = _sb_attn(sq, pad_keys(skb), pad_keys(svb), past["sb_k"], past["sb_v"], layer)
        logf_all = jnp.concatenate([jnp.transpose(past["fox_logf"], (0, 2, 1)), logf_t], axis=2)
        logf_all = jnp.pad(logf_all, ((0, 0), (0, CONV_PAD - ATT_HEADS), (0, ATT_BLOCK - n)))
        fcum = _seq_cumsum(logf_all)
        fox_o = _fox_attn(fq, pad_keys(fkb), pad_keys(fvb), past["fox_k"], past["fox_v"], fcum, layer)
        conv_hist = past["conv"]
        conv_init = jnp.pad(conv_hist, ((0, 0), (CONV_PAD - (SSD_CONV - 1), 0), (0, 0)))
        h0 = past["ssm"]
        ssd_len = n

    ssd_o, h_final = _ssd(xbc, z, small, small_t, conv_init, h0, p, layer, ssd_len)
    if conv_hist is None:
        conv_state = xbc[:, n - (SSD_CONV - 1):, :]
    else:
        conv_state = jnp.concatenate([conv_hist, xbc], axis=1)[:, -(SSD_CONV - 1):, :]

    flat = lambda a: a.reshape(m, a.shape[-1])
    h1 = _post_mix(flat(h), flat(sb_o), flat(fox_o), flat(ssd_o), p, layer, mem_k, mem_v, n, tm)
    h2 = _ffn(h1, p, layer, final_g, final_norm, tm)
    return h2.reshape(b, n, d), (skt, svt, fkt, fvt, logf_t, h_final, conv_state)


def _stacked_params(norm_mix_g, w_in, fox_b_f, conv_w, conv_b, dt_bias, a_log, d_skip, ssd_norm_g, w_out,
                    norm_x_g, mem_norm_g, wq_x, wk_x, wv_x, wo_x, norm_ffn_g, w_gate, w_up, w_down):
    aw = ATT_WIDTH
    qkv = 6 * aw
    gate_end = qkv + ATT_HEADS
    z_end = gate_end + SSD_INNER
    xbc_end = z_end + CONV_DIM
    w = jnp.transpose(w_in, (0, 2, 1))
    layers, _, d = w.shape
    row = lambda a: a.reshape(layers, 1, -1).astype(F32)
    small_pad = LANES - (ATT_HEADS + SSD_HEADS)
    zeros = lambda n: jnp.zeros((layers, n), F32)
    a_small = jnp.concatenate([zeros(ATT_HEADS), -jnp.exp(a_log.astype(F32)), zeros(small_pad)], axis=1)
    b_small = jnp.concatenate([fox_b_f.astype(F32), dt_bias.astype(F32), zeros(small_pad)], axis=1)
    col = lambda a: a[:, :2 * CONV_PAD].reshape(layers, 2 * CONV_PAD, 1)
    bf = lambda a: a.astype(BF16)
    return dict(
        norm_mix_g=row(norm_mix_g),
        wq_t=bf(jnp.concatenate([w[:, 0:aw], w[:, 3 * aw:4 * aw]], axis=1)),
        wkv_t=bf(jnp.concatenate([w[:, aw:3 * aw], w[:, 4 * aw:6 * aw], w[:, qkv:gate_end], w[:, xbc_end:],
                                  jnp.zeros((layers, 2 * CONV_PAD - (ATT_HEADS + SSD_HEADS), d), w.dtype)], axis=1)),
        wzx_t=bf(w[:, gate_end:xbc_end]),
        ws_t=bf(jnp.concatenate([w[:, qkv:gate_end], w[:, xbc_end:], jnp.zeros((layers, small_pad, d), w.dtype)],
                                axis=1)),
        b_small_row=row(b_small), b_small_col=col(b_small),
        conv_w=conv_w.astype(F32), conv_b=row(conv_b),
        a_lane=row(a_small), a_col=col(a_small),
        dskip_vec=row(jnp.repeat(d_skip, SSD_HEAD_DIM, axis=1)), ssd_norm_g=row(ssd_norm_g),
        w_out=bf(w_out), norm_x_g=row(norm_x_g), mem_norm_g=row(mem_norm_g),
        wq_x=bf(wq_x), wk_x=bf(wk_x), wv_x=bf(wv_x), wo_x=bf(wo_x), norm_ffn_g=row(norm_ffn_g),
        w_gate=bf(w_gate), w_up=bf(w_up), w_down=bf(w_down))


def kernel(x_prompt, x_sample, cache_sb_k, cache_sb_v, cache_fox_k, cache_fox_v, cache_fox_logf, state_ssm, state_conv, cache_mem_k, cache_mem_v, mem_prompt, norm_mix_g, w_in, fox_b_f, conv_w, conv_b, dt_bias, a_log, d_skip, ssd_norm_g, w_out, norm_x_g, mem_norm_g, wq_x, wk_x, wv_x, wo_x, norm_ffn_g, w_gate, w_up, w_down, final_norm_g):
    depth = w_in.shape[0]
    bp, mem_len, d = mem_prompt.shape
    bs = x_sample.shape[0]
    final_g = final_norm_g.reshape(1, d).astype(F32)
    p = _stacked_params(norm_mix_g, w_in, fox_b_f, conv_w, conv_b, dt_bias, a_log, d_skip, ssd_norm_g, w_out,
                        norm_x_g, mem_norm_g, wq_x, wk_x, wv_x, wo_x, norm_ffn_g, w_gate, w_up, w_down)
    caches_t = [_seq_minor(c) for c in (cache_sb_k, cache_sb_v, cache_fox_k, cache_fox_v)]
    hp, hs = x_prompt, x_sample
    p_mk, p_mv, p_mk_rows, p_mv_rows = _mem_kv(mem_prompt, p)
    s_mk_rows = cache_mem_k.reshape(depth, bs, mem_len, d)
    s_mv_rows = cache_mem_v.reshape(depth, bs, mem_len, d)
    p_st, s_st = [], []
    p_kv = None
    for l in range(depth):
        last = l == depth - 1
        hp, st = _layer(hp, p, l, None, p_mk_rows, p_mv_rows, final_g, last, p_kv)
        p_kv = st[:4]
        p_st.append(st)
        past = dict(sb_k=caches_t[0], sb_v=caches_t[1], fox_k=caches_t[2], fox_v=caches_t[3],
                    fox_logf=cache_fox_logf[l], ssm=state_ssm[l], conv=state_conv[l])
        hs, st = _layer(hs, p, l, past, s_mk_rows, s_mv_rows, final_g, last)
        s_st.append(st)
    stk = lambda sts, i: jnp.stack([s[i] for s in sts], axis=0)
    kv = lambda a: jnp.transpose(a.reshape(depth, a.shape[1], ATT_HEADS, HEAD_DIM, -1), (0, 1, 4, 2, 3))
    logf = lambda sts: jnp.transpose(stk(sts, 4), (0, 1, 3, 2))
    return (hp, hs,
            kv(p_kv[0]), kv(p_kv[1]), kv(p_kv[2]), kv(p_kv[3]), logf(p_st), stk(p_st, 5), stk(p_st, 6),
            p_mk, p_mv,
            kv(stk(s_st, 0)), kv(stk(s_st, 1)), kv(stk(s_st, 2)), kv(stk(s_st, 3)), logf(s_st),
            stk(s_st, 5), stk(s_st, 6))
```
